```python
import jax, jax.numpy as jnp
from jax import lax
import numpy as np

D_MODEL = 2048
BATCH = 8
SEQ = 4096
DEPTH = 2

CHUNK = 64
Q_BLOCK = 128
HEAD_DIM = 128
N_HEADS = D_MODEL // HEAD_DIM
N_FOX = N_HEADS // 2
N_SB = N_HEADS - N_FOX
N_DSA = N_HEADS
IDX_HEADS = 16
IDX_DIM = 64
TOPK_MAX = 256
D_FF = 4 * D_MODEL
ROPE_THETA = 10000.0
EPS = 1e-6
MIX_WIDTH = N_HEADS * HEAD_DIM
N_EVEN = (DEPTH + 1) // 2
N_ODD = DEPTH // 2
EVEN_SPLITS = [N_FOX * HEAD_DIM, 2 * N_FOX * HEAD_DIM, 3 * N_FOX * HEAD_DIM,
               3 * N_FOX * HEAD_DIM + N_FOX,
               3 * N_FOX * HEAD_DIM + N_FOX + N_SB * HEAD_DIM,
               3 * N_FOX * HEAD_DIM + N_FOX + 2 * N_SB * HEAD_DIM]
EVEN_WIDTH = 3 * N_FOX * HEAD_DIM + N_FOX + 3 * N_SB * HEAD_DIM
ODD_SPLITS = [N_DSA * HEAD_DIM, N_DSA * HEAD_DIM + HEAD_DIM, N_DSA * HEAD_DIM + 2 * HEAD_DIM,
              N_DSA * HEAD_DIM + 2 * HEAD_DIM + IDX_HEADS * IDX_DIM,
              N_DSA * HEAD_DIM + 2 * HEAD_DIM + IDX_HEADS * IDX_DIM + IDX_DIM]
ODD_WIDTH = N_DSA * HEAD_DIM + 2 * HEAD_DIM + IDX_HEADS * IDX_DIM + IDX_DIM + IDX_HEADS

kernel_name = "hybrid_fox_stickbreak_dsa_trunk"


def rms_norm(x, g):
    xf = x.astype(jnp.float32)
    y = xf * lax.rsqrt(jnp.mean(xf * xf, axis=-1, keepdims=True) + EPS)
    return (y * g.astype(jnp.float32)).astype(x.dtype)


def rope(x, pos):
    half = x.shape[-1] // 2
    inv = ROPE_THETA ** (-jnp.arange(half, dtype=jnp.float32) / half)
    ang = pos.astype(jnp.float32)[..., None] * inv
    cos = jnp.cos(ang)[:, :, None, :]
    sin = jnp.sin(ang)[:, :, None, :]
    xf = x.astype(jnp.float32)
    x1, x2 = xf[..., :half], xf[..., half:]
    return jnp.concatenate([x1 * cos - x2 * sin, x2 * cos + x1 * sin], axis=-1).astype(x.dtype)


def fox_sb_mixer(h, w_in, b_forget):
    B, S, _ = h.shape
    proj = h @ w_in
    fq, fk, fv, fg, sq, sk, sv = jnp.split(proj, EVEN_SPLITS, axis=-1)
    fq, fk, fv = (a.reshape(B, S, N_FOX, HEAD_DIM) for a in (fq, fk, fv))
    sq, sk, sv = (a.reshape(B, S, N_SB, HEAD_DIM) for a in (sq, sk, sv))
    log_f = jax.nn.log_sigmoid(fg.astype(jnp.float32) + b_forget.astype(jnp.float32))
    F = jnp.cumsum(log_f, axis=1).transpose(0, 2, 1)
    scale = HEAD_DIM ** -0.5
    outs_f, outs_s = [], []
    for q0 in range(0, S, Q_BLOCK):
        q1 = q0 + Q_BLOCK
        t = jnp.arange(q0, q1)[:, None]
        s = jnp.arange(q1)[None, :]
        logit = jnp.einsum('bthd,bshd->bhts', fq[:, q0:q1], fk[:, :q1]).astype(jnp.float32) * scale
        decay = F[:, :, q0:q1, None] - F[:, :, None, :q1]
        logit = jnp.where(s <= t, logit + decay, -jnp.inf)
        p = jax.nn.softmax(logit, axis=-1).astype(h.dtype)
        outs_f.append(jnp.einsum('bhts,bshd->bthd', p, fv[:, :q1]))
        z = jnp.einsum('bthd,bshd->bhts', sq[:, q0:q1], sk[:, :q1]).astype(jnp.float32) * scale
        strict = s < t
        log_1mb = jnp.where(strict, jax.nn.log_sigmoid(-z), 0.0)
        suffix = lax.cumsum(log_1mb, axis=3, reverse=True) - log_1mb
        A = jnp.where(strict, jnp.exp(jax.nn.log_sigmoid(z) + suffix), 0.0).astype(h.dtype)
        outs_s.append(jnp.einsum('bhts,bshd->bthd', A, sv[:, :q1]))
    o_f = jnp.concatenate(outs_f, axis=1).reshape(B, S, N_FOX * HEAD_DIM)
    o_s = jnp.concatenate(outs_s, axis=1).reshape(B, S, N_SB * HEAD_DIM)
    return jnp.concatenate([o_f, o_s], axis=-1)


def dsa_mixer(h, w_in, pos):
    B, S, _ = h.shape
    topk = min(TOPK_MAX, S // 4)
    proj = h @ w_in
    q, k, v, qi, ki, wi = jnp.split(proj, ODD_SPLITS, axis=-1)
    q = rope(q.reshape(B, S, N_DSA, HEAD_DIM), pos)
    k = rope(k.reshape(B, S, 1, HEAD_DIM), pos)[:, :, 0]
    qi = rope(qi.reshape(B, S, IDX_HEADS, IDX_DIM), pos)
    ki = rope(ki.reshape(B, S, 1, IDX_DIM), pos)[:, :, 0]
    wi = wi.astype(jnp.float32) * (IDX_HEADS ** -0.5)
    gather = jax.vmap(lambda tab, idx: tab[idx])
    scale = HEAD_DIM ** -0.5
    outs = []
    for q0 in range(0, S, Q_BLOCK):
        q1 = q0 + Q_BLOCK
        Lk = min(S, max(q1, topk))
        t = jnp.arange(q0, q1)
        s = jnp.arange(Lk)
        adm = (s // CHUNK)[None, :] <= (t // CHUNK)[:, None]
        isc = jnp.einsum('bthd,bsd->bths', qi[:, q0:q1], ki[:, :Lk]).astype(jnp.float32) * (IDX_DIM ** -0.5)
        I = jnp.einsum('bths,bth->bts', jax.nn.relu(isc), wi[:, q0:q1])
        I = jnp.where(adm[None], I, -jnp.inf)
        vals, idx = lax.top_k(I, topk)
        valid = jnp.isfinite(vals)
        ksel = gather(k, idx)
        vsel = gather(v, idx)
        sc = jnp.einsum('bthd,btkd->bthk', q[:, q0:q1], ksel).astype(jnp.float32) * scale
        sc = jnp.where(valid[:, :, None, :], sc, -jnp.inf)
        p = jax.nn.softmax(sc, axis=-1).astype(h.dtype)
        outs.append(jnp.einsum('bthk,btkd->bthd', p, vsel))
    return jnp.concatenate(outs, axis=1).reshape(B, S, N_DSA * HEAD_DIM)


def setup_inputs(seed: int = 0) -> dict:
    key = jax.random.key(seed)
    ks = jax.random.split(key, 16)
    f32 = jnp.float32
    x = jax.random.normal(ks[0], (BATCH, SEQ, D_MODEL), f32)
    c = jax.random.normal(ks[1], (BATCH, D_MODEL), f32)
    offset = jax.random.randint(ks[2], (BATCH, 1), 0, 4096, dtype=jnp.int32)
    positions = (offset + jnp.arange(SEQ, dtype=jnp.int32)[None, :]).astype(jnp.int32)
    ada_w = jax.random.normal(ks[3], (DEPTH, D_MODEL, 6 * D_MODEL), f32) * (0.5 * D_MODEL ** -0.5)
    ada_b = jax.random.normal(ks[4], (DEPTH, 6 * D_MODEL), f32) * 0.02
    norm_g = 1.0 + 0.1 * jax.random.normal(ks[5], (DEPTH, 4, D_MODEL), f32)
    mix_w_out = jax.random.normal(ks[6], (DEPTH, MIX_WIDTH, D_MODEL), f32) * (MIX_WIDTH ** -0.5)
    even_w_in = jax.random.normal(ks[7], (N_EVEN, D_MODEL, EVEN_WIDTH), f32) * (D_MODEL ** -0.5)
    even_b_forget = 3.0 + 0.5 * jax.random.normal(ks[8], (N_EVEN, N_FOX), f32)
    odd_w_in = jax.random.normal(ks[9], (N_ODD, D_MODEL, ODD_WIDTH), f32) * (D_MODEL ** -0.5)
    ff_w1 = jax.random.normal(ks[10], (DEPTH, D_MODEL, D_FF), f32) * (D_MODEL ** -0.5)
    ff_w2 = jax.random.normal(ks[11], (DEPTH, D_FF, D_MODEL), f32) * (D_FF ** -0.5)
    return {"x": x, "c": c, "positions": positions, "ada_w": ada_w, "ada_b": ada_b,
            "norm_g": norm_g, "mix_w_out": mix_w_out, "even_w_in": even_w_in,
            "even_b_forget": even_b_forget, "odd_w_in": odd_w_in,
            "ff_w1": ff_w1, "ff_w2": ff_w2}


def reference(x, c, positions, ada_w, ada_b, norm_g, mix_w_out, even_w_in,
              even_b_forget, odd_w_in, ff_w1, ff_w2):
    cs = jax.nn.silu(c)
    for l in range(DEPTH):
        mod = (cs @ ada_w[l] + ada_b[l])[:, None, :]
        sh_a, sc_a, g_a, sh_m, sc_m, g_m = jnp.split(mod, 6, axis=-1)
        h = rms_norm(x, norm_g[l, 0]) * (1.0 + sc_a) + sh_a
        if l % 2 == 0:
            o = fox_sb_mixer(h, even_w_in[l // 2], even_b_forget[l // 2])
        else:
            o = dsa_mixer(h, odd_w_in[l // 2], positions)
        x = x + g_a * rms_norm(o @ mix_w_out[l], norm_g[l, 1])
        h = rms_norm(x, norm_g[l, 2]) * (1.0 + sc_m) + sh_m
        y = jnp.square(jax.nn.relu(h @ ff_w1[l])) @ ff_w2[l]
        x = x + g_m * rms_norm(y, norm_g[l, 3])
    return x
```

```python
import functools

import jax
import jax.numpy as jnp
from jax import lax
from jax.experimental import pallas as pl
from jax.experimental.pallas import tpu as pltpu

F32 = jnp.float32
BF16 = jnp.bfloat16
I32 = jnp.int32

HEAD_DIM = 128
CHUNK = 64
CHUNK_SHIFT = 6
Q_BLOCK = 128
IDX_HEADS = 16
IDX_DIM = 64
TOPK_MAX = 256
ROPE_THETA = 10000.0
EPS = 1e-6

LANES = 128
INT_MIN = -(2 ** 31)
MASK_BIAS = -1e30
VMEM_LIMIT_BYTES = 56 * 1024 * 1024


def _params(n_axes, vmem=None):
    kw = dict(dimension_semantics=("arbitrary",) * n_axes)
    if vmem is not None:
        kw["vmem_limit_bytes"] = vmem
    return pltpu.CompilerParams(**kw)


def _dot_nt(a, b):
    return lax.dot_general(a, b, (((1,), (1,)), ((), ())), preferred_element_type=F32)


def _split_bf16(x, parts):
    out = []
    r = x
    for _ in range(parts):
        p = r.astype(BF16)
        out.append(p)
        r = r - p.astype(F32)
    return out


def _log_sigmoid(x):
    return jnp.minimum(x, 0.0) - jnp.log1p(jnp.exp(-jnp.abs(x)))


def _ada_kernel(c_ref, w_ref, b_ref, o_ref):
    c = c_ref[...]
    cs = c / (1.0 + jnp.exp(-c))
    o_ref[0] = jnp.dot(cs, w_ref[0], preferred_element_type=F32,
                       precision=lax.Precision.HIGHEST) + b_ref[0]


def _ada_mod(c, ada_w, ada_b):
    depth, d, n = ada_w.shape
    b = c.shape[0]
    tn = min(1024, n)
    return pl.pallas_call(
        _ada_kernel,
        grid=(depth, n // tn),
        in_specs=[pl.BlockSpec((b, d), lambda l, j: (0, 0)),
                  pl.BlockSpec((1, d, tn), lambda l, j: (l, 0, j)),
                  pl.BlockSpec((1, 1, tn), lambda l, j: (l, 0, j))],
        out_specs=pl.BlockSpec((1, b, tn), lambda l, j: (l, 0, j)),
        out_shape=jax.ShapeDtypeStruct((depth, b, n), F32),
        compiler_params=_params(2, VMEM_LIMIT_BYTES),
        name="ada_mod",
    )(c, ada_w, ada_b.reshape(depth, 1, n))


def _rms(x, g):
    ms = jnp.mean(x * x, axis=-1, keepdims=True)
    return x * lax.rsqrt(ms + EPS) * g


def _modnorm_kernel(x_ref, g_ref, mod_ref, h_ref, *, sh_row, sc_row):
    y = _rms(x_ref[...], g_ref[...])
    h = y * (1.0 + mod_ref[0, sc_row:sc_row + 1, :]) + mod_ref[0, sh_row:sh_row + 1, :]
    h_ref[...] = h.astype(h_ref.dtype)


def _modnorm(x2, g, mod, seq, sh_row, sc_row):
    t, d = x2.shape
    tm = min(512, seq)
    nsb = seq // tm
    return pl.pallas_call(
        functools.partial(_modnorm_kernel, sh_row=sh_row, sc_row=sc_row),
        grid=(t // tm,),
        in_specs=[pl.BlockSpec((tm, d), lambda i: (i, 0)),
                  pl.BlockSpec((1, d), lambda i: (0, 0)),
                  pl.BlockSpec((1, 6, d), lambda i: (i // nsb, 0, 0))],
        out_specs=pl.BlockSpec((tm, d), lambda i: (i, 0)),
        out_shape=jax.ShapeDtypeStruct((t, d), BF16),
        compiler_params=_params(1, VMEM_LIMIT_BYTES),
        name="modnorm",
    )(x2, g.reshape(1, d), mod)


def _rope128(a, cos, sin_signed):
    return a * cos + pltpu.roll(a, HEAD_DIM // 2, 1) * sin_signed


def _rope64(a, cos, sin_signed):
    lane = lax.broadcasted_iota(I32, a.shape, 1)
    first_half = (lane & (IDX_DIM - 1)) < (IDX_DIM // 2)
    rot = jnp.where(first_half, pltpu.roll(a, LANES - IDX_DIM // 2, 1), pltpu.roll(a, IDX_DIM // 2, 1))
    return a * cos + rot * sin_signed


def _mm_plain_kernel(h_ref, w_ref, o_ref):
    o_ref[...] = jnp.dot(h_ref[...], w_ref[...], preferred_element_type=F32).astype(o_ref.dtype)


def _mm_rope_kernel(h_ref, w_ref, cos_ref, sin_ref, o_ref, *, rope):
    acc = jnp.dot(h_ref[...], w_ref[...], preferred_element_type=F32)
    cos = cos_ref[...]
    sin = sin_ref[...]
    for t in range(acc.shape[1] // LANES):
        sl = slice(t * LANES, (t + 1) * LANES)
        o_ref[:, sl] = rope(acc[:, sl], cos, sin).astype(o_ref.dtype)


def _mm_kvi_kernel(h_ref, w_ref, cos_ref, sin_ref, cosi_ref, sini_ref,
                   k_ref, v_ref, kia_ref, kib_ref, wi_ref):
    acc = jnp.dot(h_ref[...], w_ref[...], preferred_element_type=F32)
    k_ref[...] = _rope128(acc[:, :LANES], cos_ref[...], sin_ref[...]).astype(k_ref.dtype)
    v_ref[...] = acc[:, LANES:2 * LANES].astype(v_ref.dtype)
    t3 = acc[:, 2 * LANES:]
    lane = lax.broadcasted_iota(I32, t3.shape, 1)
    ki = jnp.where(lane < IDX_DIM, _rope64(t3, cosi_ref[...], sini_ref[...]), 0.0)
    kia_ref[...] = ki.astype(kia_ref.dtype)
    kib_ref[...] = pltpu.roll(ki, IDX_DIM, 1).astype(kib_ref.dtype)
    wi_ref[...] = pltpu.roll(t3, IDX_DIM, 1) * (IDX_HEADS ** -0.5 * IDX_DIM ** -0.5)


def _matmul(h, w, out_dtype, tn, tm=1024, name="matmul"):
    t, k = h.shape
    n = w.shape[1]
    tm = min(tm, t)
    tn = min(tn, n)
    return pl.pallas_call(
        _mm_plain_kernel,
        grid=(t // tm, n // tn),
        in_specs=[pl.BlockSpec((tm, k), lambda i, j: (i, 0)),
                  pl.BlockSpec((k, tn), lambda i, j: (0, j))],
        out_specs=pl.BlockSpec((tm, tn), lambda i, j: (i, j)),
        out_shape=jax.ShapeDtypeStruct((t, n), out_dtype),
        compiler_params=_params(2, VMEM_LIMIT_BYTES),
        name=name,
    )(h, w)


def _matmul_rope(h, w, cos, sin, rope, tn, tm=1024, name="matmul_rope"):
    t, k = h.shape
    n = w.shape[1]
    tm = min(tm, t)
    tn = min(tn, n)
    tab = pl.BlockSpec((tm, LANES), lambda i, j: (i, 0))
    return pl.pallas_call(
        functools.partial(_mm_rope_kernel, rope=rope),
        grid=(t // tm, n // tn),
        in_specs=[pl.BlockSpec((tm, k), lambda i, j: (i, 0)),
                  pl.BlockSpec((k, tn), lambda i, j: (0, j)), tab, tab],
        out_specs=pl.BlockSpec((tm, tn), lambda i, j: (i, j)),
        out_shape=jax.ShapeDtypeStruct((t, n), BF16),
        compiler_params=_params(2, VMEM_LIMIT_BYTES),
        name=name,
    )(h, w, cos, sin)


def _matmul_kvi(h, w, cos, sin, cosi, sini, tm=1024):
    t, k = h.shape
    n = w.shape[1]
    tm = min(tm, t)
    tab = pl.BlockSpec((tm, LANES), lambda i: (i, 0))
    shp = lambda dt: jax.ShapeDtypeStruct((t, LANES), dt)
    return pl.pallas_call(
        _mm_kvi_kernel,
        grid=(t // tm,),
        in_specs=[pl.BlockSpec((tm, k), lambda i: (i, 0)),
                  pl.BlockSpec((k, n), lambda i: (0, 0)), tab, tab, tab, tab],
        out_specs=[tab] * 5,
        out_shape=[shp(BF16), shp(BF16), shp(BF16), shp(BF16), shp(F32)],
        compiler_params=_params(1, VMEM_LIMIT_BYTES),
        name="matmul_kvi",
    )(h, w, cos, sin, cosi, sini)


def _rope_tab_kernel(pos_ref, inv_ref, sgn_ref, cos_ref, sin_ref):
    ang = pos_ref[...].astype(F32) * inv_ref[...]
    cos_ref[...] = jnp.cos(ang)
    sin_ref[...] = jnp.sin(ang) * sgn_ref[...]


def _rope_tables(pos_col, half):
    t = pos_col.shape[0]
    inv = ROPE_THETA ** (-jnp.arange(half, dtype=F32) / half)
    reps = LANES // (2 * half)
    inv_row = jnp.tile(jnp.concatenate([inv, inv]), reps).reshape(1, LANES)
    sgn_row = jnp.tile(jnp.concatenate([-jnp.ones(half, F32), jnp.ones(half, F32)]), reps).reshape(1, LANES)
    tm = min(1024, t)
    row = pl.BlockSpec((1, LANES), lambda i: (0, 0))
    tab = pl.BlockSpec((tm, LANES), lambda i: (i, 0))
    return pl.pallas_call(
        _rope_tab_kernel,
        grid=(t // tm,),
        in_specs=[pl.BlockSpec((tm, 1), lambda i: (i, 0)), row, row],
        out_specs=[tab, tab],
        out_shape=[jax.ShapeDtypeStruct((t, LANES), F32)] * 2,
        compiler_params=_params(1),
        name="rope_tables",
    )(pos_col, inv_row, sgn_row)


def _gate_cumsum_kernel(fg_ref, b_ref, f_ref, carry_ref):
    @pl.when(pl.program_id(1) == 0)
    def _():
        carry_ref[...] = jnp.zeros_like(carry_ref)

    lf = _log_sigmoid(fg_ref[...] + b_ref[...])
    tc = lf.shape[0]
    r = lax.broadcasted_iota(I32, (tc, tc), 0)
    c = lax.broadcasted_iota(I32, (tc, tc), 1)
    tri = (c <= r).astype(BF16)
    cs = carry_ref[...]
    for piece in _split_bf16(lf, 3):
        cs = cs + jnp.dot(tri, piece, preferred_element_type=F32)
    f_ref[...] = cs
    carry_ref[...] = cs[tc - 1:tc, :]


def _gate_cumsum(fg, b_row, batch, seq):
    tc = min(256, seq)
    nsb = seq // tc
    return pl.pallas_call(
        _gate_cumsum_kernel,
        grid=(batch, nsb),
        in_specs=[pl.BlockSpec((tc, LANES), lambda b, j: (b * nsb + j, 0)),
                  pl.BlockSpec((1, LANES), lambda b, j: (0, 0))],
        out_specs=pl.BlockSpec((tc, LANES), lambda b, j: (b * nsb + j, 0)),
        out_shape=jax.ShapeDtypeStruct(fg.shape, F32),
        scratch_shapes=[pltpu.VMEM((1, LANES), F32)],
        compiler_params=_params(2),
        name="gate_cumsum",
    )(fg, b_row)


def _fox_kernel(q_ref, k_ref, v_ref, f_ref, o_ref, m_ref, l_ref, acc_ref, *, tq, scale):
    i = pl.program_id(2)
    q = q_ref[...]
    m_ref[...] = jnp.full_like(m_ref, -jnp.inf)
    l_ref[...] = jnp.zeros_like(l_ref)
    acc_ref[...] = jnp.zeros_like(acc_ref)

    def block(kj, masked):
        ks = pl.ds(pl.multiple_of(kj * tq, tq), tq)
        s = _dot_nt(q, k_ref[ks, :]) * scale - f_ref[0, :, ks]
        if masked:
            row = lax.broadcasted_iota(I32, s.shape, 0)
            col = lax.broadcasted_iota(I32, s.shape, 1)
            s = jnp.where(col <= row, s, -jnp.inf)
        m_prev = m_ref[...]
        m_new = jnp.maximum(m_prev, jnp.max(s, axis=-1, keepdims=True))
        alpha = jnp.exp(m_prev - m_new)
        p = jnp.exp(s - m_new)
        l_ref[...] = alpha * l_ref[...] + jnp.sum(p, axis=-1, keepdims=True)
        acc_ref[...] = alpha * acc_ref[...] + jnp.dot(p.astype(BF16), v_ref[ks, :],
                                                      preferred_element_type=F32)
        m_ref[...] = m_new

    def body(kj, carry):
        block(kj, masked=False)
        return carry

    lax.fori_loop(0, i, body, 0)
    block(i, masked=True)
    o_ref[...] = (acc_ref[...] / l_ref[...]).astype(o_ref.dtype)


def _fox_attention(qkv, f_rows, batch, seq, n_heads, q_col, k_col, v_col):
    tq = min(512, seq)
    nq = seq // tq
    t = qkv.shape[0]
    kernel = functools.partial(_fox_kernel, tq=tq, scale=HEAD_DIM ** -0.5)
    return pl.pallas_call(
        kernel,
        grid=(batch, n_heads, nq),
        in_specs=[pl.BlockSpec((tq, HEAD_DIM), lambda b, h, i: (b * nq + i, q_col + h)),
                  pl.BlockSpec((seq, HEAD_DIM), lambda b, h, i: (b, k_col + h)),
                  pl.BlockSpec((seq, HEAD_DIM), lambda b, h, i: (b, v_col + h)),
                  pl.BlockSpec((1, 1, seq), lambda b, h, i: (b * n_heads + h, 0, 0))],
        out_specs=pl.BlockSpec((tq, HEAD_DIM), lambda b, h, i: (b * nq + i, h)),
        out_shape=jax.ShapeDtypeStruct((t, n_heads * HEAD_DIM), BF16),
        scratch_shapes=[pltpu.VMEM((tq, 1), F32), pltpu.VMEM((tq, 1), F32),
                        pltpu.VMEM((tq, HEAD_DIM), F32)],
        compiler_params=_params(3, VMEM_LIMIT_BYTES),
        name="fox_attention",
    )(qkv, qkv, qkv, f_rows)


def _sb_kernel(q_ref, k_ref, v_ref, o_ref, c_ref, acc_ref, *, tq, tk, scale):
    i = pl.program_id(2)
    q = q_ref[...]
    c_ref[...] = jnp.zeros_like(c_ref)
    acc_ref[...] = jnp.zeros_like(acc_ref)
    r = lax.broadcasted_iota(I32, (tk, tk), 0)
    cc = lax.broadcasted_iota(I32, (tk, tk), 1)
    upper = (r > cc).astype(BF16)
    nblk = (i + 1) * (tq // tk)

    def body(step, carry):
        kj = nblk - 1 - step
        ks = pl.ds(pl.multiple_of(kj * tk, tk), tk)
        z = _dot_nt(q, k_ref[ks, :]) * scale
        row = lax.broadcasted_iota(I32, z.shape, 0) + i * tq
        col = lax.broadcasted_iota(I32, z.shape, 1) + kj * tk
        strict = col < row
        ls = _log_sigmoid(z)
        lneg = jnp.where(strict, ls - z, 0.0)
        suffix = jnp.zeros_like(z)
        for piece in _split_bf16(lneg, 2):
            suffix = suffix + jnp.dot(piece, upper, preferred_element_type=F32)
        c_prev = c_ref[...]
        a = jnp.where(strict, jnp.exp(ls + suffix + c_prev), 0.0)
        acc_ref[...] += jnp.dot(a.astype(BF16), v_ref[ks, :], preferred_element_type=F32)
        c_ref[...] = c_prev + suffix[:, 0:1] + lneg[:, 0:1]
        return carry

    lax.fori_loop(0, nblk, body, 0)
    o_ref[...] = acc_ref[...].astype(o_ref.dtype)


def _sb_attention(qkv, batch, seq, n_heads, q_col, k_col, v_col):
    tq = min(512, seq)
    tk = min(256, seq)
    nq = seq // tq
    t = qkv.shape[0]
    kernel = functools.partial(_sb_kernel, tq=tq, tk=tk, scale=HEAD_DIM ** -0.5)
    return pl.pallas_call(
        kernel,
        grid=(batch, n_heads, nq),
        in_specs=[pl.BlockSpec((tq, HEAD_DIM), lambda b, h, i: (b * nq + i, q_col + h)),
                  pl.BlockSpec((seq, HEAD_DIM), lambda b, h, i: (b, k_col + h)),
                  pl.BlockSpec((seq, HEAD_DIM), lambda b, h, i: (b, v_col + h))],
        out_specs=pl.BlockSpec((tq, HEAD_DIM), lambda b, h, i: (b * nq + i, h)),
        out_shape=jax.ShapeDtypeStruct((t, n_heads * HEAD_DIM), BF16),
        scratch_shapes=[pltpu.VMEM((tq, 1), F32), pltpu.VMEM((tq, HEAD_DIM), F32)],
        compiler_params=_params(3, VMEM_LIMIT_BYTES),
        name="sb_attention",
    )(qkv, qkv, qkv)


def _dsa_kernel(q_ref, qi_ref, wi_ref, k_ref, v_ref, kia_ref, kib_ref, o_ref,
                keys_ref, qs_ref, m_ref, l_ref, acc_ref, *, n_heads, kc, topk, scale):
    i = pl.program_id(1)
    tq = Q_BLOCK
    nch = ((i + 1) * tq + kc - 1) // kc
    row_t = lax.broadcasted_iota(I32, (tq, kc), 0) + i * tq
    col_l = lax.broadcasted_iota(I32, (tq, kc), 1)

    def index_body(c, carry):
        ks = pl.ds(pl.multiple_of(c * kc, kc), kc)
        kia = kia_ref[ks, :]
        kib = kib_ref[ks, :]
        wi = wi_ref[...]
        score = jnp.zeros((tq, kc), F32)
        for p in range(IDX_HEADS // 2):
            qp = qi_ref[:, p * LANES:(p + 1) * LANES]
            score = score + jnp.maximum(_dot_nt(qp, kia), 0.0) * wi[:, 2 * p:2 * p + 1]
            score = score + jnp.maximum(_dot_nt(qp, kib), 0.0) * wi[:, 2 * p + 1:2 * p + 2]
        bits = lax.bitcast_convert_type(score + 0.0, I32)
        key = bits ^ ((bits >> 31) & 0x7FFFFFFF)
        adm = ((col_l + c * kc) >> CHUNK_SHIFT) <= (row_t >> CHUNK_SHIFT)
        keys_ref[:, ks] = jnp.where(adm, key, INT_MIN)
        return carry

    lax.fori_loop(0, nch, index_body, 0)

    def count_ge(trial):
        def body(c, cnt):
            kk = keys_ref[:, pl.ds(pl.multiple_of(c * kc, kc), kc)]
            hit = jnp.where(kk >= trial, 1.0, 0.0)
            for t in range(kc // LANES):
                cnt = cnt + hit[:, t * LANES:(t + 1) * LANES]
            return cnt
        cnt = lax.fori_loop(0, nch, body, jnp.zeros((tq, LANES), F32))
        return jnp.sum(cnt, axis=-1, keepdims=True)

    kf = float(topk)
    cur = jnp.where(count_ge(jnp.zeros((tq, 1), I32)) >= kf, 0, INT_MIN).astype(I32)

    def search_body(it, cur):
        trial = cur + jnp.left_shift(jnp.int32(1), 30 - it)
        return jnp.where(count_ge(trial) >= kf, trial, cur)

    cur = lax.fori_loop(0, 31, search_body, cur)
    need = kf - count_ge(cur + 1)
    thr = jnp.maximum(cur, INT_MIN + 1)
    r = lax.broadcasted_iota(I32, (kc, kc), 0)
    cc = lax.broadcasted_iota(I32, (kc, kc), 1)
    before = (r < cc).astype(BF16)

    def tie_body(c, run):
        ks = pl.ds(pl.multiple_of(c * kc, kc), kc)
        kk = keys_ref[:, ks]
        eq = kk == cur
        eqf = jnp.where(eq, 1.0, 0.0)
        rank = jnp.dot(eqf.astype(BF16), before, preferred_element_type=F32) + run
        keys_ref[:, ks] = jnp.where(eq & (rank >= need), INT_MIN, kk)
        return run + jnp.sum(eqf, axis=-1, keepdims=True)

    lax.fori_loop(0, nch, tie_body, jnp.zeros((tq, 1), F32))

    for h in range(n_heads):
        qs_ref[h * tq:(h + 1) * tq, :] = q_ref[:, h * HEAD_DIM:(h + 1) * HEAD_DIM]
    m_ref[...] = jnp.full_like(m_ref, MASK_BIAS)
    l_ref[...] = jnp.zeros_like(l_ref)
    acc_ref[...] = jnp.zeros_like(acc_ref)

    def attn_body(c, carry):
        ks = pl.ds(pl.multiple_of(c * kc, kc), kc)
        bias = jnp.where(keys_ref[:, ks] >= thr, 0.0, MASK_BIAS)
        s = _dot_nt(qs_ref[...], k_ref[ks, :]) * scale
        s = s.reshape(n_heads, tq, kc) + bias[None]
        m_prev = m_ref[...]
        m_new = jnp.maximum(m_prev, jnp.max(s, axis=-1, keepdims=True))
        alpha = jnp.exp(m_prev - m_new)
        p = jnp.exp(s - m_new)
        l_ref[...] = alpha * l_ref[...] + jnp.sum(p, axis=-1, keepdims=True)
        pv = jnp.dot(p.reshape(n_heads * tq, kc).astype(BF16), v_ref[ks, :], preferred_element_type=F32)
        acc_ref[...] = alpha * acc_ref[...] + pv.reshape(n_heads, tq, HEAD_DIM)
        m_ref[...] = m_new
        return carry

    lax.fori_loop(0, nch, attn_body, 0)
    out = acc_ref[...] / l_ref[...]
    for h in range(n_heads):
        o_ref[:, h * HEAD_DIM:(h + 1) * HEAD_DIM] = out[h].astype(o_ref.dtype)


def _dsa_attention(q, qi, wi, k, v, kia, kib, batch, seq, n_heads):
    tq = Q_BLOCK
    kc = min(256, seq)
    nq = seq // tq
    topk = min(TOPK_MAX, seq // 4)
    t = q.shape[0]
    kernel = functools.partial(_dsa_kernel, n_heads=n_heads, kc=kc, topk=topk, scale=HEAD_DIM ** -0.5)
    qblk = lambda w: pl.BlockSpec((tq, w), lambda b, i: (b * nq + i, 0))
    full = pl.BlockSpec((seq, LANES), lambda b, i: (b, 0))
    return pl.pallas_call(
        kernel,
        grid=(batch, nq),
        in_specs=[qblk(n_heads * HEAD_DIM), qblk(IDX_HEADS * IDX_DIM), qblk(LANES), full, full, full, full],
        out_specs=qblk(n_heads * HEAD_DIM),
        out_shape=jax.ShapeDtypeStruct((t, n_heads * HEAD_DIM), BF16),
        scratch_shapes=[pltpu.VMEM((tq, seq), I32),
                        pltpu.VMEM((n_heads * tq, HEAD_DIM), BF16),
                        pltpu.VMEM((n_heads, tq, 1), F32),
                        pltpu.VMEM((n_heads, tq, 1), F32),
                        pltpu.VMEM((n_heads, tq, HEAD_DIM), F32)],
        compiler_params=_params(2, VMEM_LIMIT_BYTES),
        name="dsa_attention",
    )(q, qi, wi, k, v, kia, kib)


def _outproj_kernel(*refs, n_parts, gate_row):
    o_refs = refs[:n_parts]
    w_refs = refs[n_parts:2 * n_parts]
    x_ref, g_ref, mod_ref, out_ref = refs[2 * n_parts:]
    y = jnp.dot(o_refs[0][...], w_refs[0][...], preferred_element_type=F32)
    for o_r, w_r in zip(o_refs[1:], w_refs[1:]):
        y = y + jnp.dot(o_r[...], w_r[...], preferred_element_type=F32)
    out_ref[...] = x_ref[...] + mod_ref[0, gate_row:gate_row + 1, :] * _rms(y, g_ref[...])


def _outproj_residual(o_parts, w_parts, x2, g, mod, seq, gate_row):
    t, d = x2.shape
    tm = min(512, seq)
    nsb = seq // tm
    n_parts = len(o_parts)
    in_specs = [pl.BlockSpec((tm, o.shape[1]), lambda i: (i, 0)) for o in o_parts]
    in_specs += [pl.BlockSpec(w.shape, lambda i: (0, 0)) for w in w_parts]
    in_specs += [pl.BlockSpec((tm, d), lambda i: (i, 0)),
                 pl.BlockSpec((1, d), lambda i: (0, 0)),
                 pl.BlockSpec((1, 6, d), lambda i: (i // nsb, 0, 0))]
    return pl.pallas_call(
        functools.partial(_outproj_kernel, n_parts=n_parts, gate_row=gate_row),
        grid=(t // tm,),
        in_specs=in_specs,
        out_specs=pl.BlockSpec((tm, d), lambda i: (i, 0)),
        out_shape=jax.ShapeDtypeStruct((t, d), F32),
        compiler_params=_params(1, VMEM_LIMIT_BYTES),
        name="outproj_residual",
    )(*o_parts, *w_parts, x2, g.reshape(1, d), mod)


def _ffn_kernel(x_ref, g_in_ref, g_out_ref, mod_ref, w1_ref, w2_ref, out_ref, h_ref, acc_ref):
    j = pl.program_id(1)

    @pl.when(j == 0)
    def _():
        y = _rms(x_ref[...], g_in_ref[...])
        h_ref[...] = (y * (1.0 + mod_ref[0, 4:5, :]) + mod_ref[0, 3:4, :]).astype(h_ref.dtype)
        acc_ref[...] = jnp.zeros_like(acc_ref)

    u = jnp.maximum(jnp.dot(h_ref[...], w1_ref[...], preferred_element_type=F32), 0.0)
    acc_ref[...] += jnp.dot((u * u).astype(BF16), w2_ref[...], preferred_element_type=F32)

    @pl.when(j == pl.num_programs(1) - 1)
    def _():
        out_ref[...] = x_ref[...] + mod_ref[0, 5:6, :] * _rms(acc_ref[...], g_out_ref[...])


def _ffn_residual(x2, g_in, g_out, mod, w1, w2, seq):
    t, d = x2.shape
    f = w1.shape[1]
    tm = min(512, seq)
    tf = min(512, f)
    nsb = seq // tm
    row = pl.BlockSpec((1, d), lambda i, j: (0, 0))
    return pl.pallas_call(
        _ffn_kernel,
        grid=(t // tm, f // tf),
        in_specs=[pl.BlockSpec((tm, d), lambda i, j: (i, 0)), row, row,
                  pl.BlockSpec((1, 6, d), lambda i, j: (i // nsb, 0, 0)),
                  pl.BlockSpec((d, tf), lambda i, j: (0, j)),
                  pl.BlockSpec((tf, d), lambda i, j: (j, 0))],
        out_specs=pl.BlockSpec((tm, d), lambda i, j: (i, 0)),
        out_shape=jax.ShapeDtypeStruct((t, d), F32),
        scratch_shapes=[pltpu.VMEM((tm, d), BF16), pltpu.VMEM((tm, d), F32)],
        compiler_params=_params(2, VMEM_LIMIT_BYTES),
        name="ffn_residual",
    )(x2, g_in.reshape(1, d), g_out.reshape(1, d), mod, w1, w2)


def _even_layer(x2, mod, norm_g, w_in, b_forget, w_out, batch, seq):
    d = x2.shape[1]
    n_heads = d // HEAD_DIM
    n_fox = n_heads // 2
    n_sb = n_heads - n_fox
    fw = n_fox * HEAD_DIM
    sw = n_sb * HEAD_DIM
    w_main = jnp.concatenate([w_in[:, :3 * fw], w_in[:, 3 * fw + n_fox:]], axis=1).astype(BF16)
    w_gate = jnp.pad(w_in[:, 3 * fw:3 * fw + n_fox], ((0, 0), (0, LANES - n_fox))).astype(BF16)
    b_row = jnp.pad(b_forget.astype(F32), (0, LANES - n_fox)).reshape(1, LANES)

    h = _modnorm(x2, norm_g[0], mod, seq, sh_row=0, sc_row=1)
    qkv = _matmul(h, w_main, BF16, tn=512, name="even_in_proj")
    fg = _matmul(h, w_gate, F32, tn=LANES, name="even_gate_proj")
    f_cum = _gate_cumsum(fg, b_row, batch, seq)
    f_rows = f_cum.reshape(batch, seq, LANES)[:, :, :n_fox].transpose(0, 2, 1).reshape(batch * n_fox, 1, seq)
    o_f = _fox_attention(qkv, f_rows, batch, seq, n_fox, 0, n_fox, 2 * n_fox)
    o_s = _sb_attention(qkv, batch, seq, n_sb, 3 * n_fox, 3 * n_fox + n_sb, 3 * n_fox + 2 * n_sb)
    w_o = w_out.astype(BF16)
    return _outproj_residual([o_f, o_s], [w_o[:fw], w_o[fw:]], x2, norm_g[1], mod, seq, gate_row=2)


def _odd_layer(x2, mod, norm_g, w_in, w_out, pos_col, batch, seq):
    d = x2.shape[1]
    n_heads = d // HEAD_DIM
    qw = n_heads * HEAD_DIM
    iw = IDX_HEADS * IDX_DIM
    o_k, o_v, o_qi, o_ki, o_wi = qw, qw + HEAD_DIM, qw + 2 * HEAD_DIM, qw + 2 * HEAD_DIM + iw, qw + 2 * HEAD_DIM + iw + IDX_DIM
    w_q = w_in[:, :qw].astype(BF16)
    w_qi = w_in[:, o_qi:o_ki].astype(BF16)
    w_kvi = jnp.concatenate([w_in[:, o_k:o_qi], w_in[:, o_ki:],
                             jnp.zeros((d, LANES - IDX_DIM - IDX_HEADS), w_in.dtype)], axis=1).astype(BF16)

    cos, sin = _rope_tables(pos_col, HEAD_DIM // 2)
    cosi, sini = _rope_tables(pos_col, IDX_DIM // 2)
    h = _modnorm(x2, norm_g[0], mod, seq, sh_row=0, sc_row=1)
    q = _matmul_rope(h, w_q, cos, sin, _rope128, tn=512, name="odd_q_proj")
    qi = _matmul_rope(h, w_qi, cosi, sini, _rope64, tn=512, name="odd_qi_proj")
    k, v, kia, kib, wi = _matmul_kvi(h, w_kvi, cos, sin, cosi, sini)
    o = _dsa_attention(q, qi, wi, k, v, kia, kib, batch, seq, n_heads)
    return _outproj_residual([o], [w_out.astype(BF16)], x2, norm_g[1], mod, seq, gate_row=2)


def kernel(x, c, positions, ada_w, ada_b, norm_g, mix_w_out, even_w_in, even_b_forget, odd_w_in, ff_w1, ff_w2):
    batch, seq, d = x.shape
    depth = ada_w.shape[0]
    assert d % HEAD_DIM == 0 and seq % Q_BLOCK == 0 and seq >= TOPK_MAX
    mods = _ada_mod(c, ada_w, ada_b).reshape(depth, batch, 6, d)
    pos_col = positions.reshape(batch * seq, 1)
    x2 = x.reshape(batch * seq, d)
    for l in range(depth):
        mod = mods[l]
        if l % 2 == 0:
            x2 = _even_layer(x2, mod, norm_g[l], even_w_in[l // 2], even_b_forget[l // 2],
                             mix_w_out[l], batch, seq)
        else:
            x2 = _odd_layer(x2, mod, norm_g[l], odd_w_in[l // 2], mix_w_out[l], pos_col, batch, seq)
        x2 = _ffn_residual(x2, norm_g[l, 2], norm_g[l, 3], mod,
                           ff_w1[l].astype(BF16), ff_w2[l].astype(BF16), seq)
    return x2.reshape(batch, seq, d)
```

```python
import functools

import jax
import jax.numpy as jnp
from jax import lax
from jax.experimental import pallas as pl
from jax.experimental.pallas import tpu as pltpu

F32 = jnp.float32
BF16 = jnp.bfloat16
I32 = jnp.int32

HEAD_DIM = 128
CHUNK = 64
CHUNK_SHIFT = 6
Q_BLOCK = 128
IDX_HEADS = 16
IDX_DIM = 64
TOPK_MAX = 256
ROPE_THETA = 10000.0
EPS = 1e-6

LANES = 128
INT_MIN = -(2 ** 31)
MASK_BIAS = -1e30
SB_EXIT_LOG = -105.0
LOG2E = 1.4426950408889634
VMEM_LIMIT_BYTES = 56 * 1024 * 1024


def _params(n_axes, vmem=None):
    kw = dict(dimension_semantics=("arbitrary",) * n_axes)
    if vmem is not None:
        kw["vmem_limit_bytes"] = vmem
    return pltpu.CompilerParams(**kw)


def _dot_nt(a, b):
    return lax.dot_general(a, b, (((1,), (1,)), ((), ())), preferred_element_type=F32)


def _split_bf16(x, parts):
    out = []
    r = x
    for _ in range(parts):
        p = r.astype(BF16)
        out.append(p)
        r = r - p.astype(F32)
    return out


def _log_sigmoid(x):
    return jnp.minimum(x, 0.0) - jnp.log1p(jnp.exp(-jnp.abs(x)))


def _ada_kernel(c_ref, w_ref, b_ref, o_ref):
    c = c_ref[...]
    cs = c / (1.0 + jnp.exp(-c))
    o_ref[0] = jnp.dot(cs, w_ref[0], preferred_element_type=F32,
                       precision=lax.Precision.HIGHEST) + b_ref[0]


def _ada_mod(c, ada_w, ada_b):
    depth, d, n = ada_w.shape
    b = c.shape[0]
    tn = min(1024, n)
    return pl.pallas_call(
        _ada_kernel,
        grid=(depth, n // tn),
        in_specs=[pl.BlockSpec((b, d), lambda l, j: (0, 0)),
                  pl.BlockSpec((1, d, tn), lambda l, j: (l, 0, j)),
                  pl.BlockSpec((1, 1, tn), lambda l, j: (l, 0, j))],
        out_specs=pl.BlockSpec((1, b, tn), lambda l, j: (l, 0, j)),
        out_shape=jax.ShapeDtypeStruct((depth, b, n), F32),
        compiler_params=_params(2, VMEM_LIMIT_BYTES),
        name="ada_mod",
    )(c, ada_w, ada_b.reshape(depth, 1, n))


def _rms(x, g):
    ms = jnp.mean(x * x, axis=-1, keepdims=True)
    return x * lax.rsqrt(ms + EPS) * g


def _modnorm_kernel(x_ref, g_ref, mod_ref, h_ref, *, sh_row, sc_row):
    y = _rms(x_ref[...], g_ref[...])
    h = y * (1.0 + mod_ref[0, sc_row:sc_row + 1, :]) + mod_ref[0, sh_row:sh_row + 1, :]
    h_ref[...] = h.astype(h_ref.dtype)


def _modnorm(x2, g, mod, seq, sh_row, sc_row):
    t, d = x2.shape
    tm = min(512, seq)
    nsb = seq // tm
    return pl.pallas_call(
        functools.partial(_modnorm_kernel, sh_row=sh_row, sc_row=sc_row),
        grid=(t // tm,),
        in_specs=[pl.BlockSpec((tm, d), lambda i: (i, 0)),
                  pl.BlockSpec((1, d), lambda i: (0, 0)),
                  pl.BlockSpec((1, 6, d), lambda i: (i // nsb, 0, 0))],
        out_specs=pl.BlockSpec((tm, d), lambda i: (i, 0)),
        out_shape=jax.ShapeDtypeStruct((t, d), BF16),
        compiler_params=_params(1, VMEM_LIMIT_BYTES),
        name="modnorm",
    )(x2, g.reshape(1, d), mod)


def _rope128(a, cos, sin_signed):
    return a * cos + pltpu.roll(a, HEAD_DIM // 2, 1) * sin_signed


def _rope64(a, cos, sin_signed):
    lane = lax.broadcasted_iota(I32, a.shape, 1)
    first_half = (lane & (IDX_DIM - 1)) < (IDX_DIM // 2)
    rot = jnp.where(first_half, pltpu.roll(a, LANES - IDX_DIM // 2, 1), pltpu.roll(a, IDX_DIM // 2, 1))
    return a * cos + rot * sin_signed


def _mm_plain_kernel(h_ref, w_ref, o_ref):
    o_ref[...] = jnp.dot(h_ref[...], w_ref[...], preferred_element_type=F32).astype(o_ref.dtype)


def _mm_rope_kernel(h_ref, w_ref, cos_ref, sin_ref, o_ref, *, rope, post_scale):
    acc = jnp.dot(h_ref[...], w_ref[...], preferred_element_type=F32)
    cos = cos_ref[...] * post_scale
    sin = sin_ref[...] * post_scale
    for t in range(acc.shape[1] // LANES):
        sl = slice(t * LANES, (t + 1) * LANES)
        o_ref[:, sl] = rope(acc[:, sl], cos, sin).astype(o_ref.dtype)


def _mm_kvi_kernel(h_ref, w_ref, cos_ref, sin_ref, cosi_ref, sini_ref,
                   k_ref, v_ref, kia_ref, kib_ref, wi_ref):
    acc = jnp.dot(h_ref[...], w_ref[...], preferred_element_type=F32)
    k_ref[...] = _rope128(acc[:, :LANES], cos_ref[...], sin_ref[...]).astype(k_ref.dtype)
    v_ref[:, :LANES] = acc[:, LANES:2 * LANES].astype(v_ref.dtype)
    v_ref[:, LANES:] = jnp.ones((acc.shape[0], LANES), v_ref.dtype)
    t3 = acc[:, 2 * LANES:]
    lane = lax.broadcasted_iota(I32, t3.shape, 1)
    ki = jnp.where(lane < IDX_DIM, _rope64(t3, cosi_ref[...], sini_ref[...]), 0.0)
    kia_ref[...] = ki.astype(kia_ref.dtype)
    kib_ref[...] = pltpu.roll(ki, IDX_DIM, 1).astype(kib_ref.dtype)
    wi_ref[...] = pltpu.roll(t3, IDX_DIM, 1) * (IDX_HEADS ** -0.5 * IDX_DIM ** -0.5)


def _matmul(h, w, out_dtype, tn, tm=1024, name="matmul"):
    t, k = h.shape
    n = w.shape[1]
    tm = min(tm, t)
    tn = min(tn, n)
    return pl.pallas_call(
        _mm_plain_kernel,
        grid=(t // tm, n // tn),
        in_specs=[pl.BlockSpec((tm, k), lambda i, j: (i, 0)),
                  pl.BlockSpec((k, tn), lambda i, j: (0, j))],
        out_specs=pl.BlockSpec((tm, tn), lambda i, j: (i, j)),
        out_shape=jax.ShapeDtypeStruct((t, n), out_dtype),
        compiler_params=_params(2, VMEM_LIMIT_BYTES),
        name=name,
    )(h, w)


def _matmul_rope(h, w, cos, sin, rope, tn, tm=1024, post_scale=1.0, name="matmul_rope"):
    t, k = h.shape
    n = w.shape[1]
    tm = min(tm, t)
    tn = min(tn, n)
    tab = pl.BlockSpec((tm, LANES), lambda i, j: (i, 0))
    return pl.pallas_call(
        functools.partial(_mm_rope_kernel, rope=rope, post_scale=post_scale),
        grid=(t // tm, n // tn),
        in_specs=[pl.BlockSpec((tm, k), lambda i, j: (i, 0)),
                  pl.BlockSpec((k, tn), lambda i, j: (0, j)), tab, tab],
        out_specs=pl.BlockSpec((tm, tn), lambda i, j: (i, j)),
        out_shape=jax.ShapeDtypeStruct((t, n), BF16),
        compiler_params=_params(2, VMEM_LIMIT_BYTES),
        name=name,
    )(h, w, cos, sin)


def _matmul_kvi(h, w, cos, sin, cosi, sini, tm=1024):
    t, k = h.shape
    n = w.shape[1]
    tm = min(tm, t)
    tab = pl.BlockSpec((tm, LANES), lambda i: (i, 0))
    shp = lambda dt: jax.ShapeDtypeStruct((t, LANES), dt)
    return pl.pallas_call(
        _mm_kvi_kernel,
        grid=(t // tm,),
        in_specs=[pl.BlockSpec((tm, k), lambda i: (i, 0)),
                  pl.BlockSpec((k, n), lambda i: (0, 0)), tab, tab, tab, tab],
        out_specs=[tab, pl.BlockSpec((tm, 2 * LANES), lambda i: (i, 0)), tab, tab, tab],
        out_shape=[shp(BF16), jax.ShapeDtypeStruct((t, 2 * LANES), BF16), shp(BF16), shp(BF16), shp(F32)],
        compiler_params=_params(1, VMEM_LIMIT_BYTES),
        name="matmul_kvi",
    )(h, w, cos, sin, cosi, sini)


def _rope_tab_kernel(pos_ref, inv_ref, sgn_ref, cos_ref, sin_ref):
    ang = pos_ref[...].astype(F32) * inv_ref[...]
    cos_ref[...] = jnp.cos(ang)
    sin_ref[...] = jnp.sin(ang) * sgn_ref[...]


def _rope_tables(pos_col, half):
    t = pos_col.shape[0]
    inv = ROPE_THETA ** (-jnp.arange(half, dtype=F32) / half)
    reps = LANES // (2 * half)
    inv_row = jnp.tile(jnp.concatenate([inv, inv]), reps).reshape(1, LANES)
    sgn_row = jnp.tile(jnp.concatenate([-jnp.ones(half, F32), jnp.ones(half, F32)]), reps).reshape(1, LANES)
    tm = min(1024, t)
    row = pl.BlockSpec((1, LANES), lambda i: (0, 0))
    tab = pl.BlockSpec((tm, LANES), lambda i: (i, 0))
    return pl.pallas_call(
        _rope_tab_kernel,
        grid=(t // tm,),
        in_specs=[pl.BlockSpec((tm, 1), lambda i: (i, 0)), row, row],
        out_specs=[tab, tab],
        out_shape=[jax.ShapeDtypeStruct((t, LANES), F32)] * 2,
        compiler_params=_params(1),
        name="rope_tables",
    )(pos_col, inv_row, sgn_row)


def _gate_cumsum_kernel(fg_ref, b_ref, f_ref, carry_ref):
    @pl.when(pl.program_id(1) == 0)
    def _():
        carry_ref[...] = jnp.zeros_like(carry_ref)

    lf = _log_sigmoid(fg_ref[...] + b_ref[...])
    tc = lf.shape[0]
    r = lax.broadcasted_iota(I32, (tc, tc), 0)
    c = lax.broadcasted_iota(I32, (tc, tc), 1)
    tri = (c <= r).astype(BF16)
    cs = carry_ref[...]
    for piece in _split_bf16(lf, 3):
        cs = cs + jnp.dot(tri, piece, preferred_element_type=F32)
    f_ref[...] = cs * LOG2E
    carry_ref[...] = cs[tc - 1:tc, :]


def _gate_cumsum(fg, b_row, batch, seq):
    tc = min(256, seq)
    nsb = seq // tc
    return pl.pallas_call(
        _gate_cumsum_kernel,
        grid=(batch, nsb),
        in_specs=[pl.BlockSpec((tc, LANES), lambda b, j: (b * nsb + j, 0)),
                  pl.BlockSpec((1, LANES), lambda b, j: (0, 0))],
        out_specs=pl.BlockSpec((tc, LANES), lambda b, j: (b * nsb + j, 0)),
        out_shape=jax.ShapeDtypeStruct(fg.shape, F32),
        scratch_shapes=[pltpu.VMEM((1, LANES), F32)],
        compiler_params=_params(2),
        name="gate_cumsum",
    )(fg, b_row)


def _softmax_block(tiles, m_prev):
    mx = functools.reduce(jnp.maximum, tiles)
    m_new = jnp.maximum(m_prev, jnp.broadcast_to(jnp.max(mx, axis=-1, keepdims=True), mx.shape))
    alpha = jnp.exp2(m_prev - m_new)
    return m_new, alpha, [jnp.exp2(t - m_new) for t in tiles]


def _fox_kernel(q_ref, k_ref, v_ref, f_ref, o_ref, m_ref, l_ref, acc_ref, *, tq):
    i = pl.program_id(2)
    q = q_ref[...]
    m_ref[...] = jnp.full_like(m_ref, -jnp.inf)
    l_ref[...] = jnp.zeros_like(l_ref)
    acc_ref[...] = jnp.zeros_like(acc_ref)
    n_tiles = tq // LANES

    def block(kj, masked):
        ks = pl.ds(pl.multiple_of(kj * tq, tq), tq)
        s = _dot_nt(q, k_ref[ks, :]) - f_ref[0, :, ks]
        if masked:
            row = lax.broadcasted_iota(I32, s.shape, 0)
            col = lax.broadcasted_iota(I32, s.shape, 1)
            s = jnp.where(col <= row, s, -jnp.inf)
        tiles = [s[:, t * LANES:(t + 1) * LANES] for t in range(n_tiles)]
        m_new, alpha, p = _softmax_block(tiles, m_ref[...])
        psum = functools.reduce(jnp.add, p)
        l_ref[...] = alpha * l_ref[...] + jnp.broadcast_to(jnp.sum(psum, axis=-1, keepdims=True), psum.shape)
        pv = jnp.dot(jnp.concatenate(p, axis=-1).astype(BF16), v_ref[ks, :], preferred_element_type=F32)
        acc_ref[...] = alpha * acc_ref[...] + pv
        m_ref[...] = m_new

    def body(kj, carry):
        block(kj, masked=False)
        return carry

    lax.fori_loop(0, i, body, 0)
    block(i, masked=True)
    o_ref[...] = (acc_ref[...] / l_ref[...]).astype(o_ref.dtype)


def _fox_attention(qkv, f_rows, batch, seq, n_heads, q_col, k_col, v_col):
    tq = min(512, seq)
    nq = seq // tq
    t = qkv.shape[0]
    kernel = functools.partial(_fox_kernel, tq=tq)
    return pl.pallas_call(
        kernel,
        grid=(batch, n_heads, nq),
        in_specs=[pl.BlockSpec((tq, HEAD_DIM), lambda b, h, i: (b * nq + i, q_col + h)),
                  pl.BlockSpec((seq, HEAD_DIM), lambda b, h, i: (b, k_col + h)),
                  pl.BlockSpec((seq, HEAD_DIM), lambda b, h, i: (b, v_col + h)),
                  pl.BlockSpec((1, 1, seq), lambda b, h, i: (b * n_heads + h, 0, 0))],
        out_specs=pl.BlockSpec((tq, HEAD_DIM), lambda b, h, i: (b * nq + i, h)),
        out_shape=jax.ShapeDtypeStruct((t, n_heads * HEAD_DIM), BF16),
        scratch_shapes=[pltpu.VMEM((tq, LANES), F32), pltpu.VMEM((tq, LANES), F32),
                        pltpu.VMEM((tq, HEAD_DIM), F32)],
        compiler_params=_params(3, VMEM_LIMIT_BYTES),
        name="fox_attention",
    )(qkv, qkv, qkv, f_rows)


def _sb_kernel(q_ref, k_ref, v_ref, o_ref, c_ref, acc_ref, *, tq, tk):
    i = pl.program_id(2)
    q = q_ref[...]
    c_ref[...] = jnp.zeros_like(c_ref)
    acc_ref[...] = jnp.zeros_like(acc_ref)
    r = lax.broadcasted_iota(I32, (tk, tk), 0)
    cc = lax.broadcasted_iota(I32, (tk, tk), 1)
    upper = (r > cc).astype(BF16)
    nblk = (i + 1) * (tq // tk)

    def cond(carry):
        step, c_max = carry
        return jnp.logical_and(step < nblk, c_max > SB_EXIT_LOG)

    def body(carry):
        step, _ = carry
        kj = nblk - 1 - step
        ks = pl.ds(pl.multiple_of(kj * tk, tk), tk)
        z = _dot_nt(q, k_ref[ks, :])
        row = lax.broadcasted_iota(I32, z.shape, 0) + i * tq
        col = lax.broadcasted_iota(I32, z.shape, 1) + kj * tk
        strict = col < row
        ls = _log_sigmoid(z)
        lneg = jnp.where(strict, ls - z, 0.0)
        suffix = jnp.zeros_like(z)
        for piece in _split_bf16(lneg, 2):
            suffix = suffix + jnp.dot(piece, upper, preferred_element_type=F32)
        c_prev = c_ref[...]
        a = jnp.where(strict, jnp.exp(ls + suffix + c_prev), 0.0)
        acc_ref[...] += jnp.dot(a.astype(BF16), v_ref[ks, :], preferred_element_type=F32)
        c_new = c_prev + suffix[:, 0:1] + lneg[:, 0:1]
        c_ref[...] = c_new
        return step + 1, jnp.max(c_new)

    lax.while_loop(cond, body, (jnp.int32(0), jnp.float32(0.0)))
    o_ref[...] = acc_ref[...].astype(o_ref.dtype)


def _sb_attention(qkv, batch, seq, n_heads, q_col, k_col, v_col):
    tq = min(512, seq)
    tk = min(256, seq)
    nq = seq // tq
    t = qkv.shape[0]
    kernel = functools.partial(_sb_kernel, tq=tq, tk=tk)
    return pl.pallas_call(
        kernel,
        grid=(batch, n_heads, nq),
        in_specs=[pl.BlockSpec((tq, HEAD_DIM), lambda b, h, i: (b * nq + i, q_col + h)),
                  pl.BlockSpec((seq, HEAD_DIM), lambda b, h, i: (b, k_col + h)),
                  pl.BlockSpec((seq, HEAD_DIM), lambda b, h, i: (b, v_col + h))],
        out_specs=pl.BlockSpec((tq, HEAD_DIM), lambda b, h, i: (b * nq + i, h)),
        out_shape=jax.ShapeDtypeStruct((t, n_heads * HEAD_DIM), BF16),
        scratch_shapes=[pltpu.VMEM((tq, 1), F32), pltpu.VMEM((tq, HEAD_DIM), F32)],
        compiler_params=_params(3, VMEM_LIMIT_BYTES),
        name="sb_attention",
    )(qkv, qkv, qkv)


def _dsa_kernel(q_ref, qi_ref, wi_ref, k_ref, v_ref, kia_ref, kib_ref, o_ref,
                keys_ref, qs_ref, m_ref, acc_ref, *, n_heads, kc, kca, topk):
    i = pl.program_id(1)
    tq = Q_BLOCK
    nch = ((i + 1) * tq + kc - 1) // kc
    row_t = lax.broadcasted_iota(I32, (tq, kc), 0) + i * tq
    col_l = lax.broadcasted_iota(I32, (tq, kc), 1)

    def index_body(c, carry):
        ks = pl.ds(pl.multiple_of(c * kc, kc), kc)
        kia = kia_ref[ks, :]
        kib = kib_ref[ks, :]
        wi = wi_ref[...]
        score = jnp.zeros((tq, kc), F32)
        for p in range(IDX_HEADS // 2):
            qp = qi_ref[:, p * LANES:(p + 1) * LANES]
            score = score + jnp.maximum(_dot_nt(qp, kia), 0.0) * wi[:, 2 * p:2 * p + 1]
            score = score + jnp.maximum(_dot_nt(qp, kib), 0.0) * wi[:, 2 * p + 1:2 * p + 2]
        bits = lax.bitcast_convert_type(score + 0.0, I32)
        key = bits ^ ((bits >> 31) & 0x7FFFFFFF)
        adm = ((col_l + c * kc) >> CHUNK_SHIFT) <= (row_t >> CHUNK_SHIFT)
        keys_ref[:, ks] = jnp.where(adm, key, INT_MIN)
        return carry

    lax.fori_loop(0, nch, index_body, 0)

    def count_ge(trial):
        def body(c, cnt):
            kk = keys_ref[:, pl.ds(pl.multiple_of(c * kc, kc), kc)]
            hit = jnp.where(kk >= trial, 1.0, 0.0)
            for t in range(kc // LANES):
                cnt = cnt + hit[:, t * LANES:(t + 1) * LANES]
            return cnt
        cnt = lax.fori_loop(0, nch, body, jnp.zeros((tq, LANES), F32))
        return jnp.sum(cnt, axis=-1, keepdims=True)

    kf = float(topk)
    cur = jnp.where(count_ge(jnp.zeros((tq, 1), I32)) >= kf, 0, INT_MIN).astype(I32)

    def search_body(it, cur):
        trial = cur + jnp.left_shift(jnp.int32(1), 30 - it)
        return jnp.where(count_ge(trial) >= kf, trial, cur)

    cur = lax.fori_loop(0, 31, search_body, cur)
    need = kf - count_ge(cur + 1)
    thr = jnp.maximum(cur, INT_MIN + 1)
    r = lax.broadcasted_iota(I32, (kc, kc), 0)
    cc = lax.broadcasted_iota(I32, (kc, kc), 1)
    before = (r < cc).astype(BF16)

    def tie_body(c, run):
        ks = pl.ds(pl.multiple_of(c * kc, kc), kc)
        kk = keys_ref[:, ks]
        eq = kk == cur
        eqf = jnp.where(eq, 1.0, 0.0)
        rank = jnp.dot(eqf.astype(BF16), before, preferred_element_type=F32) + run
        keys_ref[:, ks] = jnp.where(eq & (rank >= need), INT_MIN, kk)
        return run + jnp.sum(eqf, axis=-1, keepdims=True)

    lax.fori_loop(0, nch, tie_body, jnp.zeros((tq, 1), F32))

    for h in range(n_heads):
        qs_ref[h * tq:(h + 1) * tq, :] = q_ref[:, h * HEAD_DIM:(h + 1) * HEAD_DIM]
    m_ref[...] = jnp.full_like(m_ref, MASK_BIAS)
    acc_ref[...] = jnp.zeros_like(acc_ref)
    n_att = ((i + 1) * tq + kca - 1) // kca
    n_tiles = kca // LANES
    scored = nch * kc

    def attn_body(c, carry):
        ks = pl.ds(pl.multiple_of(c * kca, kca), kca)
        col = lax.broadcasted_iota(I32, (tq, kca), 1) + c * kca
        bias = jnp.where((keys_ref[:, ks] >= thr) & (col < scored), 0.0, MASK_BIAS)
        s = _dot_nt(qs_ref[...], k_ref[ks, :]).reshape(n_heads, tq, kca)
        tiles = [s[:, :, t * LANES:(t + 1) * LANES] + bias[None, :, t * LANES:(t + 1) * LANES]
                 for t in range(n_tiles)]
        m_new, alpha, p = _softmax_block(tiles, m_ref[...])
        pb = jnp.concatenate(p, axis=-1).reshape(n_heads * tq, kca).astype(BF16)
        pv = jnp.dot(pb, v_ref[ks, :], preferred_element_type=F32)
        a2 = alpha.reshape(n_heads * tq, LANES)
        acc_ref[...] = jnp.concatenate([a2, a2], axis=-1) * acc_ref[...] + pv
        m_ref[...] = m_new
        return carry

    lax.fori_loop(0, n_att, attn_body, 0)
    acc = acc_ref[...]
    out = acc[:, :HEAD_DIM] / acc[:, HEAD_DIM:]
    for h in range(n_heads):
        o_ref[:, h * HEAD_DIM:(h + 1) * HEAD_DIM] = out[h * tq:(h + 1) * tq, :].astype(o_ref.dtype)


def _dsa_attention(q, qi, wi, k, v, kia, kib, batch, seq, n_heads):
    tq = Q_BLOCK
    kc = min(256, seq)
    nq = seq // tq
    topk = min(TOPK_MAX, seq // 4)
    t = q.shape[0]
    kca = min(512, seq)
    kernel = functools.partial(_dsa_kernel, n_heads=n_heads, kc=kc, kca=kca, topk=topk)
    qblk = lambda w: pl.BlockSpec((tq, w), lambda b, i: (b * nq + i, 0))
    full = pl.BlockSpec((seq, LANES), lambda b, i: (b, 0))
    vext = pl.BlockSpec((seq, 2 * LANES), lambda b, i: (b, 0))
    return pl.pallas_call(
        kernel,
        grid=(batch, nq),
        in_specs=[qblk(n_heads * HEAD_DIM), qblk(IDX_HEADS * IDX_DIM), qblk(LANES), full, vext, full, full],
        out_specs=qblk(n_heads * HEAD_DIM),
        out_shape=jax.ShapeDtypeStruct((t, n_heads * HEAD_DIM), BF16),
        scratch_shapes=[pltpu.VMEM((tq, seq), I32),
                        pltpu.VMEM((n_heads * tq, HEAD_DIM), BF16),
                        pltpu.VMEM((n_heads, tq, LANES), F32),
                        pltpu.VMEM((n_heads * tq, 2 * LANES), F32)],
        compiler_params=_params(2, VMEM_LIMIT_BYTES),
        name="dsa_attention",
    )(q, qi, wi, k, v, kia, kib)


def _outproj_kernel(*refs, n_parts, gate_row):
    o_refs = refs[:n_parts]
    w_refs = refs[n_parts:2 * n_parts]
    x_ref, g_ref, mod_ref, out_ref = refs[2 * n_parts:]
    y = jnp.dot(o_refs[0][...], w_refs[0][...], preferred_element_type=F32)
    for o_r, w_r in zip(o_refs[1:], w_refs[1:]):
        y = y + jnp.dot(o_r[...], w_r[...], preferred_element_type=F32)
    out_ref[...] = x_ref[...] + mod_ref[0, gate_row:gate_row + 1, :] * _rms(y, g_ref[...])


def _outproj_residual(o_parts, w_parts, x2, g, mod, seq, gate_row):
    t, d = x2.shape
    tm = min(512, seq)
    nsb = seq // tm
    n_parts = len(o_parts)
    in_specs = [pl.BlockSpec((tm, o.shape[1]), lambda i: (i, 0)) for o in o_parts]
    in_specs += [pl.BlockSpec(w.shape, lambda i: (0, 0)) for w in w_parts]
    in_specs += [pl.BlockSpec((tm, d), lambda i: (i, 0)),
                 pl.BlockSpec((1, d), lambda i: (0, 0)),
                 pl.BlockSpec((1, 6, d), lambda i: (i // nsb, 0, 0))]
    return pl.pallas_call(
        functools.partial(_outproj_kernel, n_parts=n_parts, gate_row=gate_row),
        grid=(t // tm,),
        in_specs=in_specs,
        out_specs=pl.BlockSpec((tm, d), lambda i: (i, 0)),
        out_shape=jax.ShapeDtypeStruct((t, d), F32),
        compiler_params=_params(1, VMEM_LIMIT_BYTES),
        name="outproj_residual",
    )(*o_parts, *w_parts, x2, g.reshape(1, d), mod)


def _ffn_kernel(x_ref, g_in_ref, g_out_ref, mod_ref, w1_ref, w2_ref, out_ref, h_ref, acc_ref):
    j = pl.program_id(1)

    @pl.when(j == 0)
    def _():
        y = _rms(x_ref[...], g_in_ref[...])
        h_ref[...] = (y * (1.0 + mod_ref[0, 4:5, :]) + mod_ref[0, 3:4, :]).astype(h_ref.dtype)
        acc_ref[...] = jnp.zeros_like(acc_ref)

    u = jnp.maximum(jnp.dot(h_ref[...], w1_ref[...], preferred_element_type=F32), 0.0)
    acc_ref[...] += jnp.dot((u * u).astype(BF16), w2_ref[...], preferred_element_type=F32)

    @pl.when(j == pl.num_programs(1) - 1)
    def _():
        out_ref[...] = x_ref[...] + mod_ref[0, 5:6, :] * _rms(acc_ref[...], g_out_ref[...])


def _ffn_residual(x2, g_in, g_out, mod, w1, w2, seq):
    t, d = x2.shape
    f = w1.shape[1]
    tm = min(512, seq)
    tf = min(512, f)
    nsb = seq // tm
    row = pl.BlockSpec((1, d), lambda i, j: (0, 0))
    return pl.pallas_call(
        _ffn_kernel,
        grid=(t // tm, f // tf),
        in_specs=[pl.BlockSpec((tm, d), lambda i, j: (i, 0)), row, row,
                  pl.BlockSpec((1, 6, d), lambda i, j: (i // nsb, 0, 0)),
                  pl.BlockSpec((d, tf), lambda i, j: (0, j)),
                  pl.BlockSpec((tf, d), lambda i, j: (j, 0))],
        out_specs=pl.BlockSpec((tm, d), lambda i, j: (i, 0)),
        out_shape=jax.ShapeDtypeStruct((t, d), F32),
        scratch_shapes=[pltpu.VMEM((tm, d), BF16), pltpu.VMEM((tm, d), F32)],
        compiler_params=_params(2, VMEM_LIMIT_BYTES),
        name="ffn_residual",
    )(x2, g_in.reshape(1, d), g_out.reshape(1, d), mod, w1, w2)


def _even_layer(x2, mod, norm_g, w_in, b_forget, w_out, batch, seq):
    d = x2.shape[1]
    n_heads = d // HEAD_DIM
    n_fox = n_heads // 2
    n_sb = n_heads - n_fox
    fw = n_fox * HEAD_DIM
    sw = n_sb * HEAD_DIM
    scale = HEAD_DIM ** -0.5
    w_main = jnp.concatenate([w_in[:, :fw] * (scale * LOG2E), w_in[:, fw:3 * fw],
                              w_in[:, 3 * fw + n_fox:3 * fw + n_fox + sw] * scale,
                              w_in[:, 3 * fw + n_fox + sw:]], axis=1).astype(BF16)
    w_gate = jnp.pad(w_in[:, 3 * fw:3 * fw + n_fox], ((0, 0), (0, LANES - n_fox))).astype(BF16)
    b_row = jnp.pad(b_forget.astype(F32), (0, LANES - n_fox)).reshape(1, LANES)

    h = _modnorm(x2, norm_g[0], mod, seq, sh_row=0, sc_row=1)
    qkv = _matmul(h, w_main, BF16, tn=512, name="even_in_proj")
    fg = _matmul(h, w_gate, F32, tn=LANES, name="even_gate_proj")
    f_cum = _gate_cumsum(fg, b_row, batch, seq)
    f_rows = f_cum.reshape(batch, seq, LANES)[:, :, :n_fox].transpose(0, 2, 1).reshape(batch * n_fox, 1, seq)
    o_f = _fox_attention(qkv, f_rows, batch, seq, n_fox, 0, n_fox, 2 * n_fox)
    o_s = _sb_attention(qkv, batch, seq, n_sb, 3 * n_fox, 3 * n_fox + n_sb, 3 * n_fox + 2 * n_sb)
    w_o = w_out.astype(BF16)
    return _outproj_residual([o_f, o_s], [w_o[:fw], w_o[fw:]], x2, norm_g[1], mod, seq, gate_row=2)


def _odd_layer(x2, mod, norm_g, w_in, w_out, pos_col, batch, seq):
    d = x2.shape[1]
    n_heads = d // HEAD_DIM
    qw = n_heads * HEAD_DIM
    iw = IDX_HEADS * IDX_DIM
    o_k, o_v, o_qi, o_ki, o_wi = qw, qw + HEAD_DIM, qw + 2 * HEAD_DIM, qw + 2 * HEAD_DIM + iw, qw + 2 * HEAD_DIM + iw + IDX_DIM
    w_q = w_in[:, :qw].astype(BF16)
    w_qi = w_in[:, o_qi:o_ki].astype(BF16)
    w_kvi = jnp.concatenate([w_in[:, o_k:o_qi], w_in[:, o_ki:],
                             jnp.zeros((d, LANES - IDX_DIM - IDX_HEADS), w_in.dtype)], axis=1).astype(BF16)

    cos, sin = _rope_tables(pos_col, HEAD_DIM // 2)
    cosi, sini = _rope_tables(pos_col, IDX_DIM // 2)
    h = _modnorm(x2, norm_g[0], mod, seq, sh_row=0, sc_row=1)
    q = _matmul_rope(h, w_q, cos, sin, _rope128, tn=512, post_scale=HEAD_DIM ** -0.5 * LOG2E, name="odd_q_proj")
    qi = _matmul_rope(h, w_qi, cosi, sini, _rope64, tn=512, name="odd_qi_proj")
    k, v, kia, kib, wi = _matmul_kvi(h, w_kvi, cos, sin, cosi, sini)
    o = _dsa_attention(q, qi, wi, k, v, kia, kib, batch, seq, n_heads)
    return _outproj_residual([o], [w_out.astype(BF16)], x2, norm_g[1], mod, seq, gate_row=2)


def kernel(x, c, positions, ada_w, ada_b, norm_g, mix_w_out, even_w_in, even_b_forget, odd_w_in, ff_w1, ff_w2):
    batch, seq, d = x.shape
    depth = ada_w.shape[0]
    assert d % HEAD_DIM == 0 and seq % Q_BLOCK == 0 and seq >= TOPK_MAX
    mods = _ada_mod(c, ada_w, ada_b).reshape(depth, batch, 6, d)
    pos_col = positions.reshape(batch * seq, 1)
    x2 = x.reshape(batch * seq, d)
    for l in range(depth):
        mod = mods[l]
        if l % 2 == 0:
            x2 = _even_layer(x2, mod, norm_g[l], even_w_in[l // 2], even_b_forget[l // 2],
                             mix_w_out[l], batch, seq)
        else:
            x2 = _odd_layer(x2, mod, norm_g[l], odd_w_in[l // 2], mix_w_out[l], pos_col, batch, seq)
        x2 = _ffn_residual(x2, norm_g[l, 2], norm_g[l, 3], mod,
                           ff_w1[l].astype(BF16), ff_w2[l].astype(BF16), seq)
    return x2.reshape(batch, seq, d)
```

```python
import functools

import jax
import jax.numpy as jnp
from jax import lax
from jax.experimental import pallas as pl
from jax.experimental.pallas import tpu as pltpu

F32 = jnp.float32
BF16 = jnp.bfloat16
I32 = jnp.int32

HEAD_DIM = 128
CHUNK = 64
CHUNK_SHIFT = 6
Q_BLOCK = 128
IDX_HEADS = 16
IDX_DIM = 64
TOPK_MAX = 256
ROPE_THETA = 10000.0
EPS = 1e-6

LANES = 128
INT_MIN = -(2 ** 31)
MASK_BIAS = -1e30
SB_EXIT_LOG = -105.0
LOG2E = 1.4426950408889634
VMEM_LIMIT_BYTES = 56 * 1024 * 1024


def _params(n_axes, vmem=None):
    kw = dict(dimension_semantics=("arbitrary",) * n_axes)
    if vmem is not None:
        kw["vmem_limit_bytes"] = vmem
    return pltpu.CompilerParams(**kw)


def _dot_nt(a, b):
    return lax.dot_general(a, b, (((1,), (1,)), ((), ())), preferred_element_type=F32)


def _split_bf16(x, parts):
    out = []
    r = x
    for _ in range(parts):
        p = r.astype(BF16)
        out.append(p)
        r = r - p.astype(F32)
    return out


def _log_sigmoid(x):
    return jnp.minimum(x, 0.0) - jnp.log1p(jnp.exp(-jnp.abs(x)))


def _ada_kernel(c_ref, w_ref, b_ref, o_ref):
    c = c_ref[...]
    cs = c / (1.0 + jnp.exp(-c))
    o_ref[0] = jnp.dot(cs, w_ref[0], preferred_element_type=F32,
                       precision=lax.Precision.HIGHEST) + b_ref[0]


def _ada_mod(c, ada_w, ada_b):
    depth, d, n = ada_w.shape
    b = c.shape[0]
    tn = min(1024, n)
    return pl.pallas_call(
        _ada_kernel,
        grid=(depth, n // tn),
        in_specs=[pl.BlockSpec((b, d), lambda l, j: (0, 0)),
                  pl.BlockSpec((1, d, tn), lambda l, j: (l, 0, j)),
                  pl.BlockSpec((1, 1, tn), lambda l, j: (l, 0, j))],
        out_specs=pl.BlockSpec((1, b, tn), lambda l, j: (l, 0, j)),
        out_shape=jax.ShapeDtypeStruct((depth, b, n), F32),
        compiler_params=_params(2, VMEM_LIMIT_BYTES),
        name="ada_mod",
    )(c, ada_w, ada_b.reshape(depth, 1, n))


def _rms(x, g):
    ms = jnp.mean(x * x, axis=-1, keepdims=True)
    return x * lax.rsqrt(ms + EPS) * g


def _modnorm_kernel(x_ref, g_ref, mod_ref, h_ref, *, sh_row, sc_row):
    y = _rms(x_ref[...], g_ref[...])
    h = y * (1.0 + mod_ref[0, sc_row:sc_row + 1, :]) + mod_ref[0, sh_row:sh_row + 1, :]
    h_ref[...] = h.astype(h_ref.dtype)


def _modnorm(x2, g, mod, seq, sh_row, sc_row):
    t, d = x2.shape
    tm = min(512, seq)
    nsb = seq // tm
    return pl.pallas_call(
        functools.partial(_modnorm_kernel, sh_row=sh_row, sc_row=sc_row),
        grid=(t // tm,),
        in_specs=[pl.BlockSpec((tm, d), lambda i: (i, 0)),
                  pl.BlockSpec((1, d), lambda i: (0, 0)),
                  pl.BlockSpec((1, 6, d), lambda i: (i // nsb, 0, 0))],
        out_specs=pl.BlockSpec((tm, d), lambda i: (i, 0)),
        out_shape=jax.ShapeDtypeStruct((t, d), BF16),
        compiler_params=_params(1, VMEM_LIMIT_BYTES),
        name="modnorm",
    )(x2, g.reshape(1, d), mod)


def _rope128(a, cos, sin_signed):
    return a * cos + pltpu.roll(a, HEAD_DIM // 2, 1) * sin_signed


def _rope64(a, cos, sin_signed):
    lane = lax.broadcasted_iota(I32, a.shape, 1)
    first_half = (lane & (IDX_DIM - 1)) < (IDX_DIM // 2)
    rot = jnp.where(first_half, pltpu.roll(a, LANES - IDX_DIM // 2, 1), pltpu.roll(a, IDX_DIM // 2, 1))
    return a * cos + rot * sin_signed


def _mm_plain_kernel(h_ref, w_ref, o_ref):
    o_ref[...] = jnp.dot(h_ref[...], w_ref[...], preferred_element_type=F32).astype(o_ref.dtype)


def _mm_rope_kernel(h_ref, w_ref, cos_ref, sin_ref, o_ref, *, rope, post_scale):
    acc = jnp.dot(h_ref[...], w_ref[...], preferred_element_type=F32)
    cos = cos_ref[...] * post_scale
    sin = sin_ref[...] * post_scale
    for t in range(acc.shape[1] // LANES):
        sl = slice(t * LANES, (t + 1) * LANES)
        o_ref[:, sl] = rope(acc[:, sl], cos, sin).astype(o_ref.dtype)


def _mm_kvi_kernel(h_ref, w_ref, cos_ref, sin_ref, cosi_ref, sini_ref,
                   k_ref, v_ref, kia_ref, kib_ref, wi_ref):
    acc = jnp.dot(h_ref[...], w_ref[...], preferred_element_type=F32)
    k_ref[...] = _rope128(acc[:, :LANES], cos_ref[...], sin_ref[...]).astype(k_ref.dtype)
    v_ref[:, :LANES] = acc[:, LANES:2 * LANES].astype(v_ref.dtype)
    v_ref[:, LANES:] = jnp.ones((acc.shape[0], LANES), v_ref.dtype)
    t3 = acc[:, 2 * LANES:]
    lane = lax.broadcasted_iota(I32, t3.shape, 1)
    ki = jnp.where(lane < IDX_DIM, _rope64(t3, cosi_ref[...], sini_ref[...]), 0.0)
    kia_ref[...] = ki.astype(kia_ref.dtype)
    kib_ref[...] = pltpu.roll(ki, IDX_DIM, 1).astype(kib_ref.dtype)
    wi_ref[...] = pltpu.roll(t3, IDX_DIM, 1) * (IDX_HEADS ** -0.5 * IDX_DIM ** -0.5)


def _matmul(h, w, out_dtype, tn, tm=1024, name="matmul"):
    t, k = h.shape
    n = w.shape[1]
    tm = min(tm, t)
    tn = min(tn, n)
    return pl.pallas_call(
        _mm_plain_kernel,
        grid=(t // tm, n // tn),
        in_specs=[pl.BlockSpec((tm, k), lambda i, j: (i, 0)),
                  pl.BlockSpec((k, tn), lambda i, j: (0, j))],
        out_specs=pl.BlockSpec((tm, tn), lambda i, j: (i, j)),
        out_shape=jax.ShapeDtypeStruct((t, n), out_dtype),
        compiler_params=_params(2, VMEM_LIMIT_BYTES),
        name=name,
    )(h, w)


def _matmul_rope(h, w, cos, sin, rope, tn, tm=1024, post_scale=1.0, name="matmul_rope"):
    t, k = h.shape
    n = w.shape[1]
    tm = min(tm, t)
    tn = min(tn, n)
    tab = pl.BlockSpec((tm, LANES), lambda i, j: (i, 0))
    return pl.pallas_call(
        functools.partial(_mm_rope_kernel, rope=rope, post_scale=post_scale),
        grid=(t // tm, n // tn),
        in_specs=[pl.BlockSpec((tm, k), lambda i, j: (i, 0)),
                  pl.BlockSpec((k, tn), lambda i, j: (0, j)), tab, tab],
        out_specs=pl.BlockSpec((tm, tn), lambda i, j: (i, j)),
        out_shape=jax.ShapeDtypeStruct((t, n), BF16),
        compiler_params=_params(2, VMEM_LIMIT_BYTES),
        name=name,
    )(h, w, cos, sin)


def _matmul_kvi(h, w, cos, sin, cosi, sini, tm=1024):
    t, k = h.shape
    n = w.shape[1]
    tm = min(tm, t)
    tab = pl.BlockSpec((tm, LANES), lambda i: (i, 0))
    shp = lambda dt: jax.ShapeDtypeStruct((t, LANES), dt)
    return pl.pallas_call(
        _mm_kvi_kernel,
        grid=(t // tm,),
        in_specs=[pl.BlockSpec((tm, k), lambda i: (i, 0)),
                  pl.BlockSpec((k, n), lambda i: (0, 0)), tab, tab, tab, tab],
        out_specs=[tab, pl.BlockSpec((tm, 2 * LANES), lambda i: (i, 0)), tab, tab, tab],
        out_shape=[shp(BF16), jax.ShapeDtypeStruct((t, 2 * LANES), BF16), shp(BF16), shp(BF16), shp(F32)],
        compiler_params=_params(1, VMEM_LIMIT_BYTES),
        name="matmul_kvi",
    )(h, w, cos, sin, cosi, sini)


def _rope_tab_kernel(pos_ref, inv_ref, sgn_ref, cos_ref, sin_ref):
    ang = pos_ref[...].astype(F32) * inv_ref[...]
    cos_ref[...] = jnp.cos(ang)
    sin_ref[...] = jnp.sin(ang) * sgn_ref[...]


def _rope_tables(pos_col, half):
    t = pos_col.shape[0]
    inv = ROPE_THETA ** (-jnp.arange(half, dtype=F32) / half)
    reps = LANES // (2 * half)
    inv_row = jnp.tile(jnp.concatenate([inv, inv]), reps).reshape(1, LANES)
    sgn_row = jnp.tile(jnp.concatenate([-jnp.ones(half, F32), jnp.ones(half, F32)]), reps).reshape(1, LANES)
    tm = min(1024, t)
    row = pl.BlockSpec((1, LANES), lambda i: (0, 0))
    tab = pl.BlockSpec((tm, LANES), lambda i: (i, 0))
    return pl.pallas_call(
        _rope_tab_kernel,
        grid=(t // tm,),
        in_specs=[pl.BlockSpec((tm, 1), lambda i: (i, 0)), row, row],
        out_specs=[tab, tab],
        out_shape=[jax.ShapeDtypeStruct((t, LANES), F32)] * 2,
        compiler_params=_params(1),
        name="rope_tables",
    )(pos_col, inv_row, sgn_row)


def _gate_cumsum_kernel(fg_ref, b_ref, f_ref, carry_ref):
    @pl.when(pl.program_id(1) == 0)
    def _():
        carry_ref[...] = jnp.zeros_like(carry_ref)

    lf = _log_sigmoid(fg_ref[...] + b_ref[...])
    tc = lf.shape[0]
    r = lax.broadcasted_iota(I32, (tc, tc), 0)
    c = lax.broadcasted_iota(I32, (tc, tc), 1)
    tri = (c <= r).astype(BF16)
    cs = carry_ref[...]
    for piece in _split_bf16(lf, 3):
        cs = cs + jnp.dot(tri, piece, preferred_element_type=F32)
    f_ref[...] = cs * LOG2E
    carry_ref[...] = cs[tc - 1:tc, :]


def _gate_cumsum(fg, b_row, batch, seq):
    tc = min(256, seq)
    nsb = seq // tc
    return pl.pallas_call(
        _gate_cumsum_kernel,
        grid=(batch, nsb),
        in_specs=[pl.BlockSpec((tc, LANES), lambda b, j: (b * nsb + j, 0)),
                  pl.BlockSpec((1, LANES), lambda b, j: (0, 0))],
        out_specs=pl.BlockSpec((tc, LANES), lambda b, j: (b * nsb + j, 0)),
        out_shape=jax.ShapeDtypeStruct(fg.shape, F32),
        scratch_shapes=[pltpu.VMEM((1, LANES), F32)],
        compiler_params=_params(2),
        name="gate_cumsum",
    )(fg, b_row)


def _softmax_block(tiles, m_prev):
    mx = functools.reduce(jnp.maximum, tiles)
    m_new = jnp.maximum(m_prev, jnp.broadcast_to(jnp.max(mx, axis=-1, keepdims=True), mx.shape))
    alpha = jnp.exp2(m_prev - m_new)
    return m_new, alpha, [jnp.exp2(t - m_new) for t in tiles]


def _fox_kernel(q_ref, k_ref, v_ref, f_ref, o_ref, m_ref, l_ref, acc_ref, sa_ref, sb_ref, *, tq):
    i = pl.program_id(2)
    m_ref[...] = jnp.full_like(m_ref, -jnp.inf)
    l_ref[...] = jnp.zeros_like(l_ref)
    acc_ref[...] = jnp.zeros_like(acc_ref)
    n_tiles = tq // LANES

    def keys(kj):
        return pl.ds(pl.multiple_of(kj * tq, tq), tq)

    def scores(kj, s_ref):
        s_ref[...] = _dot_nt(q_ref[...], k_ref[keys(kj), :])

    def softmax_pv(kj, s_ref, masked):
        ks = keys(kj)
        s = s_ref[...] - f_ref[0, :, ks]
        if masked:
            row = lax.broadcasted_iota(I32, s.shape, 0)
            col = lax.broadcasted_iota(I32, s.shape, 1)
            s = jnp.where(col <= row, s, -jnp.inf)
        tiles = [s[:, t * LANES:(t + 1) * LANES] for t in range(n_tiles)]
        m_new, alpha, p = _softmax_block(tiles, m_ref[...])
        psum = functools.reduce(jnp.add, p)
        l_ref[...] = alpha * l_ref[...] + jnp.broadcast_to(jnp.sum(psum, axis=-1, keepdims=True), psum.shape)
        pv = jnp.dot(jnp.concatenate(p, axis=-1).astype(BF16), v_ref[ks, :], preferred_element_type=F32)
        acc_ref[...] = alpha * acc_ref[...] + pv
        m_ref[...] = m_new

    scores(0, sa_ref)

    def pair(p, carry):
        scores(2 * p + 1, sb_ref)
        softmax_pv(2 * p, sa_ref, masked=False)
        scores(2 * p + 2, sa_ref)
        softmax_pv(2 * p + 1, sb_ref, masked=False)
        return carry

    lax.fori_loop(0, i // 2, pair, 0)

    @pl.when(i % 2 == 0)
    def _():
        softmax_pv(i, sa_ref, masked=True)

    @pl.when(i % 2 == 1)
    def _():
        scores(i, sb_ref)
        softmax_pv(i - 1, sa_ref, masked=False)
        softmax_pv(i, sb_ref, masked=True)

    o_ref[...] = (acc_ref[...] / l_ref[...]).astype(o_ref.dtype)


def _fox_attention(qkv, f_rows, batch, seq, n_heads, q_col, k_col, v_col):
    tq = min(512, seq)
    nq = seq // tq
    t = qkv.shape[0]
    kernel = functools.partial(_fox_kernel, tq=tq)
    return pl.pallas_call(
        kernel,
        grid=(batch, n_heads, nq),
        in_specs=[pl.BlockSpec((tq, HEAD_DIM), lambda b, h, i: (b * nq + i, q_col + h)),
                  pl.BlockSpec((seq, HEAD_DIM), lambda b, h, i: (b, k_col + h)),
                  pl.BlockSpec((seq, HEAD_DIM), lambda b, h, i: (b, v_col + h)),
                  pl.BlockSpec((1, 1, seq), lambda b, h, i: (b * n_heads + h, 0, 0))],
        out_specs=pl.BlockSpec((tq, HEAD_DIM), lambda b, h, i: (b * nq + i, h)),
        out_shape=jax.ShapeDtypeStruct((t, n_heads * HEAD_DIM), BF16),
        scratch_shapes=[pltpu.VMEM((tq, LANES), F32), pltpu.VMEM((tq, LANES), F32),
                        pltpu.VMEM((tq, HEAD_DIM), F32),
                        pltpu.VMEM((tq, tq), F32), pltpu.VMEM((tq, tq), F32)],
        compiler_params=_params(3, VMEM_LIMIT_BYTES),
        name="fox_attention",
    )(qkv, qkv, qkv, f_rows)


def _sb_kernel(q_ref, k_ref, v_ref, o_ref, c_ref, acc_ref, *, tq, tk):
    i = pl.program_id(2)
    q = q_ref[...]
    c_ref[...] = jnp.zeros_like(c_ref)
    acc_ref[...] = jnp.zeros_like(acc_ref)
    r = lax.broadcasted_iota(I32, (tk, tk), 0)
    cc = lax.broadcasted_iota(I32, (tk, tk), 1)
    upper = (r > cc).astype(BF16)
    nblk = (i + 1) * (tq // tk)

    def cond(carry):
        step, c_max = carry
        return jnp.logical_and(step < nblk, c_max > SB_EXIT_LOG)

    def body(carry):
        step, _ = carry
        kj = nblk - 1 - step
        ks = pl.ds(pl.multiple_of(kj * tk, tk), tk)
        z = _dot_nt(q, k_ref[ks, :])
        row = lax.broadcasted_iota(I32, z.shape, 0) + i * tq
        col = lax.broadcasted_iota(I32, z.shape, 1) + kj * tk
        strict = col < row
        ls = _log_sigmoid(z)
        lneg = jnp.where(strict, ls - z, 0.0)
        suffix = jnp.zeros_like(z)
        for piece in _split_bf16(lneg, 2):
            suffix = suffix + jnp.dot(piece, upper, preferred_element_type=F32)
        c_prev = c_ref[...]
        a = jnp.where(strict, jnp.exp(ls + suffix + c_prev), 0.0)
        acc_ref[...] += jnp.dot(a.astype(BF16), v_ref[ks, :], preferred_element_type=F32)
        c_new = c_prev + suffix[:, 0:1] + lneg[:, 0:1]
        c_ref[...] = c_new
        return step + 1, jnp.max(c_new)

    lax.while_loop(cond, body, (jnp.int32(0), jnp.float32(0.0)))
    o_ref[...] = acc_ref[...].astype(o_ref.dtype)


def _sb_attention(qkv, batch, seq, n_heads, q_col, k_col, v_col):
    tq = min(512, seq)
    tk = min(256, seq)
    nq = seq // tq
    t = qkv.shape[0]
    kernel = functools.partial(_sb_kernel, tq=tq, tk=tk)
    return pl.pallas_call(
        kernel,
        grid=(batch, n_heads, nq),
        in_specs=[pl.BlockSpec((tq, HEAD_DIM), lambda b, h, i: (b * nq + i, q_col + h)),
                  pl.BlockSpec((seq, HEAD_DIM), lambda b, h, i: (b, k_col + h)),
                  pl.BlockSpec((seq, HEAD_DIM), lambda b, h, i: (b, v_col + h))],
        out_specs=pl.BlockSpec((tq, HEAD_DIM), lambda b, h, i: (b * nq + i, h)),
        out_shape=jax.ShapeDtypeStruct((t, n_heads * HEAD_DIM), BF16),
        scratch_shapes=[pltpu.VMEM((tq, 1), F32), pltpu.VMEM((tq, HEAD_DIM), F32)],
        compiler_params=_params(3, VMEM_LIMIT_BYTES),
        name="sb_attention",
    )(qkv, qkv, qkv)


def _dsa_kernel(q_ref, qi_ref, wi_ref, k_ref, v_ref, kia_ref, kib_ref, o_ref,
                keys_ref, qs_ref, m_ref, acc_ref, sa_ref, sb_ref, *, n_heads, kc, kca, topk):
    i = pl.program_id(1)
    tq = Q_BLOCK
    nch = ((i + 1) * tq + kc - 1) // kc
    row_t = lax.broadcasted_iota(I32, (tq, kc), 0) + i * tq
    col_l = lax.broadcasted_iota(I32, (tq, kc), 1)

    def index_body(c, carry):
        ks = pl.ds(pl.multiple_of(c * kc, kc), kc)
        kia = kia_ref[ks, :]
        kib = kib_ref[ks, :]
        wi = wi_ref[...]
        score = jnp.zeros((tq, kc), F32)
        for p in range(IDX_HEADS // 2):
            qp = qi_ref[:, p * LANES:(p + 1) * LANES]
            score = score + jnp.maximum(_dot_nt(qp, kia), 0.0) * wi[:, 2 * p:2 * p + 1]
            score = score + jnp.maximum(_dot_nt(qp, kib), 0.0) * wi[:, 2 * p + 1:2 * p + 2]
        bits = lax.bitcast_convert_type(score + 0.0, I32)
        key = bits ^ ((bits >> 31) & 0x7FFFFFFF)
        adm = ((col_l + c * kc) >> CHUNK_SHIFT) <= (row_t >> CHUNK_SHIFT)
        keys_ref[:, ks] = jnp.where(adm, key, INT_MIN)
        return carry

    lax.fori_loop(0, nch, index_body, 0)

    def count_ge(trial):
        def body(c, cnt):
            kk = keys_ref[:, pl.ds(pl.multiple_of(c * kc, kc), kc)]
            for t in range(kc // LANES):
                cnt = cnt + jnp.where(kk[:, t * LANES:(t + 1) * LANES] >= trial, 1.0, 0.0)
            return cnt
        cnt = lax.fori_loop(0, nch, body, jnp.zeros((tq, LANES), F32))
        return jnp.broadcast_to(jnp.sum(cnt, axis=-1, keepdims=True), cnt.shape)

    kf = float(topk)
    cur = jnp.where(count_ge(jnp.zeros((tq, LANES), I32)) >= kf, 0, INT_MIN).astype(I32)

    def search_body(it, cur):
        trial = cur + jnp.left_shift(jnp.int32(1), 30 - it)
        return jnp.where(count_ge(trial) >= kf, trial, cur)

    cur = lax.fori_loop(0, 31, search_body, cur)
    thr = jnp.maximum(cur, INT_MIN + 1)

    surplus = jnp.where((count_ge(cur) > kf) & (cur > INT_MIN), 1.0, 0.0)

    @pl.when(jnp.max(surplus) > 0.0)
    def _():
        need = kf - count_ge(cur + 1)
        r = lax.broadcasted_iota(I32, (kc, kc), 0)
        cc = lax.broadcasted_iota(I32, (kc, kc), 1)
        before = (r < cc).astype(BF16)
        lane_tiles = [slice(t * LANES, (t + 1) * LANES) for t in range(kc // LANES)]

        def tie_body(c, run):
            base = pl.multiple_of(c * kc, kc)
            kk = keys_ref[:, pl.ds(base, kc)]
            eq = [kk[:, sl] == cur for sl in lane_tiles]
            eqf = [jnp.where(e, 1.0, 0.0) for e in eq]
            rank = jnp.dot(jnp.concatenate(eqf, axis=-1).astype(BF16), before, preferred_element_type=F32)
            for t, sl in enumerate(lane_tiles):
                retire = eq[t] & (rank[:, sl] + run >= need)
                keys_ref[:, pl.ds(base + t * LANES, LANES)] = jnp.where(retire, INT_MIN, kk[:, sl])
            tot = functools.reduce(jnp.add, eqf)
            return run + jnp.broadcast_to(jnp.sum(tot, axis=-1, keepdims=True), tot.shape)

        lax.fori_loop(0, nch, tie_body, jnp.zeros((tq, LANES), F32))

    for h in range(n_heads):
        qs_ref[h * tq:(h + 1) * tq, :] = q_ref[:, h * HEAD_DIM:(h + 1) * HEAD_DIM]
    m_ref[...] = jnp.full_like(m_ref, MASK_BIAS)
    acc_ref[...] = jnp.zeros_like(acc_ref)
    n_att = ((i + 1) * tq + kca - 1) // kca
    n_tiles = kca // LANES
    scored = nch * kc

    def chunk(c):
        return pl.ds(pl.multiple_of(c * kca, kca), kca)

    def scores(c, s_ref):
        s_ref[...] = _dot_nt(qs_ref[...], k_ref[chunk(jnp.minimum(c, n_att - 1)), :])

    def softmax_pv(c, s_ref):
        ks = chunk(c)
        lane = lax.broadcasted_iota(I32, (tq, LANES), 1)
        kk = keys_ref[:, ks]
        s = s_ref[...].reshape(n_heads, tq, kca)
        tiles = []
        for t in range(n_tiles):
            sl = slice(t * LANES, (t + 1) * LANES)
            bias = jnp.where((kk[:, sl] >= thr) & (lane < scored - c * kca - t * LANES), 0.0, MASK_BIAS)
            tiles.append(s[:, :, sl] + bias[None])
        m_new, alpha, p = _softmax_block(tiles, m_ref[...])
        pb = jnp.concatenate(p, axis=-1).reshape(n_heads * tq, kca).astype(BF16)
        pv = jnp.dot(pb, v_ref[ks, :], preferred_element_type=F32)
        a2 = alpha.reshape(n_heads * tq, LANES)
        acc_ref[...] = jnp.concatenate([a2, a2], axis=-1) * acc_ref[...] + pv
        m_ref[...] = m_new

    scores(0, sa_ref)

    def pair(p, carry):
        scores(2 * p + 1, sb_ref)
        softmax_pv(2 * p, sa_ref)
        scores(2 * p + 2, sa_ref)
        softmax_pv(2 * p + 1, sb_ref)
        return carry

    lax.fori_loop(0, n_att // 2, pair, 0)

    @pl.when(n_att % 2 == 1)
    def _():
        softmax_pv(n_att - 1, sa_ref)

    acc = acc_ref[...]
    out = acc[:, :HEAD_DIM] / acc[:, HEAD_DIM:]
    for h in range(n_heads):
        o_ref[:, h * HEAD_DIM:(h + 1) * HEAD_DIM] = out[h * tq:(h + 1) * tq, :].astype(o_ref.dtype)


def _dsa_attention(q, qi, wi, k, v, kia, kib, batch, seq, n_heads):
    tq = Q_BLOCK
    kc = min(256, seq)
    nq = seq // tq
    topk = min(TOPK_MAX, seq // 4)
    t = q.shape[0]
    kca = min(512, seq)
    kernel = functools.partial(_dsa_kernel, n_heads=n_heads, kc=kc, kca=kca, topk=topk)
    qblk = lambda w: pl.BlockSpec((tq, w), lambda b, i: (b * nq + i, 0))
    full = pl.BlockSpec((seq, LANES), lambda b, i: (b, 0))
    vext = pl.BlockSpec((seq, 2 * LANES), lambda b, i: (b, 0))
    return pl.pallas_call(
        kernel,
        grid=(batch, nq),
        in_specs=[qblk(n_heads * HEAD_DIM), qblk(IDX_HEADS * IDX_DIM), qblk(LANES), full, vext, full, full],
        out_specs=qblk(n_heads * HEAD_DIM),
        out_shape=jax.ShapeDtypeStruct((t, n_heads * HEAD_DIM), BF16),
        scratch_shapes=[pltpu.VMEM((tq, seq), I32),
                        pltpu.VMEM((n_heads * tq, HEAD_DIM), BF16),
                        pltpu.VMEM((n_heads, tq, LANES), F32),
                        pltpu.VMEM((n_heads * tq, 2 * LANES), F32),
                        pltpu.VMEM((n_heads * tq, kca), F32),
                        pltpu.VMEM((n_heads * tq, kca), F32)],
        compiler_params=_params(2, VMEM_LIMIT_BYTES),
        name="dsa_attention",
    )(q, qi, wi, k, v, kia, kib)


def _outproj_kernel(*refs, n_parts, gate_row):
    o_refs = refs[:n_parts]
    w_refs = refs[n_parts:2 * n_parts]
    x_ref, g_ref, mod_ref, out_ref = refs[2 * n_parts:]
    y = jnp.dot(o_refs[0][...], w_refs[0][...], preferred_element_type=F32)
    for o_r, w_r in zip(o_refs[1:], w_refs[1:]):
        y = y + jnp.dot(o_r[...], w_r[...], preferred_element_type=F32)
    out_ref[...] = x_ref[...] + mod_ref[0, gate_row:gate_row + 1, :] * _rms(y, g_ref[...])


def _outproj_residual(o_parts, w_parts, x2, g, mod, seq, gate_row):
    t, d = x2.shape
    tm = min(512, seq)
    nsb = seq // tm
    n_parts = len(o_parts)
    in_specs = [pl.BlockSpec((tm, o.shape[1]), lambda i: (i, 0)) for o in o_parts]
    in_specs += [pl.BlockSpec(w.shape, lambda i: (0, 0)) for w in w_parts]
    in_specs += [pl.BlockSpec((tm, d), lambda i: (i, 0)),
                 pl.BlockSpec((1, d), lambda i: (0, 0)),
                 pl.BlockSpec((1, 6, d), lambda i: (i // nsb, 0, 0))]
    return pl.pallas_call(
        functools.partial(_outproj_kernel, n_parts=n_parts, gate_row=gate_row),
        grid=(t // tm,),
        in_specs=in_specs,
        out_specs=pl.BlockSpec((tm, d), lambda i: (i, 0)),
        out_shape=jax.ShapeDtypeStruct((t, d), F32),
        compiler_params=_params(1, VMEM_LIMIT_BYTES),
        name="outproj_residual",
    )(*o_parts, *w_parts, x2, g.reshape(1, d), mod)


def _ffn_kernel(x_ref, g_in_ref, g_out_ref, mod_ref, w1_ref, w2_ref, out_ref, h_ref, acc_ref):
    j = pl.program_id(1)

    @pl.when(j == 0)
    def _():
        y = _rms(x_ref[...], g_in_ref[...])
        h_ref[...] = (y * (1.0 + mod_ref[0, 4:5, :]) + mod_ref[0, 3:4, :]).astype(h_ref.dtype)
        acc_ref[...] = jnp.zeros_like(acc_ref)

    u = jnp.maximum(jnp.dot(h_ref[...], w1_ref[...], preferred_element_type=F32), 0.0)
    acc_ref[...] += jnp.dot((u * u).astype(BF16), w2_ref[...], preferred_element_type=F32)

    @pl.when(j == pl.num_programs(1) - 1)
    def _():
        out_ref[...] = x_ref[...] + mod_ref[0, 5:6, :] * _rms(acc_ref[...], g_out_ref[...])


def _ffn_residual(x2, g_in, g_out, mod, w1, w2, seq):
    t, d = x2.shape
    f = w1.shape[1]
    tm = min(512, seq)
    tf = min(1024, f)
    nsb = seq // tm
    row = pl.BlockSpec((1, d), lambda i, j: (0, 0))
    return pl.pallas_call(
        _ffn_kernel,
        grid=(t // tm, f // tf),
        in_specs=[pl.BlockSpec((tm, d), lambda i, j: (i, 0)), row, row,
                  pl.BlockSpec((1, 6, d), lambda i, j: (i // nsb, 0, 0)),
                  pl.BlockSpec((d, tf), lambda i, j: (0, j)),
                  pl.BlockSpec((tf, d), lambda i, j: (j, 0))],
        out_specs=pl.BlockSpec((tm, d), lambda i, j: (i, 0)),
        out_shape=jax.ShapeDtypeStruct((t, d), F32),
        scratch_shapes=[pltpu.VMEM((tm, d), BF16), pltpu.VMEM((tm, d), F32)],
        compiler_params=_params(2, VMEM_LIMIT_BYTES),
        name="ffn_residual",
    )(x2, g_in.reshape(1, d), g_out.reshape(1, d), mod, w1, w2)


def _even_layer(x2, mod, norm_g, w_in, b_forget, w_out, batch, seq):
    d = x2.shape[1]
    n_heads = d // HEAD_DIM
    n_fox = n_heads // 2
    n_sb = n_heads - n_fox
    fw = n_fox * HEAD_DIM
    sw = n_sb * HEAD_DIM
    scale = HEAD_DIM ** -0.5
    w_main = jnp.concatenate([w_in[:, :fw] * (scale * LOG2E), w_in[:, fw:3 * fw],
                              w_in[:, 3 * fw + n_fox:3 * fw + n_fox + sw] * scale,
                              w_in[:, 3 * fw + n_fox + sw:]], axis=1).astype(BF16)
    w_gate = jnp.pad(w_in[:, 3 * fw:3 * fw + n_fox], ((0, 0), (0, LANES - n_fox))).astype(BF16)
    b_row = jnp.pad(b_forget.astype(F32), (0, LANES - n_fox)).reshape(1, LANES)

    h = _modnorm(x2, norm_g[0], mod, seq, sh_row=0, sc_row=1)
    qkv = _matmul(h, w_main, BF16, tn=512, name="even_in_proj")
    fg = _matmul(h, w_gate, F32, tn=LANES, name="even_gate_proj")
    f_cum = _gate_cumsum(fg, b_row, batch, seq)
    f_rows = f_cum.reshape(batch, seq, LANES)[:, :, :n_fox].transpose(0, 2, 1).reshape(batch * n_fox, 1, seq)
    o_f = _fox_attention(qkv, f_rows, batch, seq, n_fox, 0, n_fox, 2 * n_fox)
    o_s = _sb_attention(qkv, batch, seq, n_sb, 3 * n_fox, 3 * n_fox + n_sb, 3 * n_fox + 2 * n_sb)
    w_o = w_out.astype(BF16)
    return _outproj_residual([o_f, o_s], [w_o[:fw], w_o[fw:]], x2, norm_g[1], mod, seq, gate_row=2)


def _odd_layer(x2, mod, norm_g, w_in, w_out, pos_col, batch, seq):
    d = x2.shape[1]
    n_heads = d // HEAD_DIM
    qw = n_heads * HEAD_DIM
    iw = IDX_HEADS * IDX_DIM
    o_k, o_v, o_qi, o_ki, o_wi = qw, qw + HEAD_DIM, qw + 2 * HEAD_DIM, qw + 2 * HEAD_DIM + iw, qw + 2 * HEAD_DIM + iw + IDX_DIM
    w_q = w_in[:, :qw].astype(BF16)
    w_qi = w_in[:, o_qi:o_ki].astype(BF16)
    w_kvi = jnp.concatenate([w_in[:, o_k:o_qi], w_in[:, o_ki:],
                             jnp.zeros((d, LANES - IDX_DIM - IDX_HEADS), w_in.dtype)], axis=1).astype(BF16)

    cos, sin = _rope_tables(pos_col, HEAD_DIM // 2)
    cosi, sini = _rope_tables(pos_col, IDX_DIM // 2)
    h = _modnorm(x2, norm_g[0], mod, seq, sh_row=0, sc_row=1)
    q = _matmul_rope(h, w_q, cos, sin, _rope128, tn=512, post_scale=HEAD_DIM ** -0.5 * LOG2E, name="odd_q_proj")
    qi = _matmul_rope(h, w_qi, cosi, sini, _rope64, tn=512, name="odd_qi_proj")
    k, v, kia, kib, wi = _matmul_kvi(h, w_kvi, cos, sin, cosi, sini)
    o = _dsa_attention(q, qi, wi, k, v, kia, kib, batch, seq, n_heads)
    return _outproj_residual([o], [w_out.astype(BF16)], x2, norm_g[1], mod, seq, gate_row=2)


def kernel(x, c, positions, ada_w, ada_b, norm_g, mix_w_out, even_w_in, even_b_forget, odd_w_in, ff_w1, ff_w2):
    batch, seq, d = x.shape
    depth = ada_w.shape[0]
    assert d % HEAD_DIM == 0 and seq % Q_BLOCK == 0 and seq >= TOPK_MAX
    mods = _ada_mod(c, ada_w, ada_b).reshape(depth, batch, 6, d)
    pos_col = positions.reshape(batch * seq, 1)
    x2 = x.reshape(batch * seq, d)
    for l in range(depth):
        mod = mods[l]
        if l % 2 == 0:
            x2 = _even_layer(x2, mod, norm_g[l], even_w_in[l // 2], even_b_forget[l // 2],
                             mix_w_out[l], batch, seq)
        else:
            x2 = _odd_layer(x2, mod, norm_g[l], odd_w_in[l // 2], mix_w_out[l], pos_col, batch, seq)
        x2 = _ffn_residual(x2, norm_g[l, 2], norm_g[l, 3], mod,
                           ff_w1[l].astype(BF16), ff_w2[l].astype(BF16), seq)
    return x2.reshape(batch, seq, d)
```

```python
import functools

import jax
import jax.numpy as jnp
from jax import lax
from jax.experimental import pallas as pl
from jax.experimental.pallas import tpu as pltpu

F32 = jnp.float32
BF16 = jnp.bfloat16
I32 = jnp.int32

HEAD_DIM = 128
CHUNK = 64
CHUNK_SHIFT = 6
Q_BLOCK = 128
IDX_HEADS = 16
IDX_DIM = 64
TOPK_MAX = 256
ROPE_THETA = 10000.0
EPS = 1e-6

LANES = 128
SUBLANES = 8
COUNT_ROWS = 4 * SUBLANES
INT_MIN = -(2 ** 31)
MASK_BIAS = -1e30
SB_EXIT_LOG = -105.0
LOG2E = 1.4426950408889634
VMEM_LIMIT_BYTES = 56 * 1024 * 1024


def _params(n_axes, vmem=None):
    kw = dict(dimension_semantics=("arbitrary",) * n_axes)
    if vmem is not None:
        kw["vmem_limit_bytes"] = vmem
    return pltpu.CompilerParams(**kw)


def _dot_nt(a, b):
    return lax.dot_general(a, b, (((1,), (1,)), ((), ())), preferred_element_type=F32)


def _split_bf16(x, parts):
    out = []
    r = x
    for _ in range(parts):
        p = r.astype(BF16)
        out.append(p)
        r = r - p.astype(F32)
    return out


def _log_sigmoid(x):
    return jnp.minimum(x, 0.0) - jnp.log1p(jnp.exp(-jnp.abs(x)))


def _ada_kernel(c_ref, w_ref, b_ref, o_ref):
    c = c_ref[...]
    cs = c / (1.0 + jnp.exp(-c))
    o_ref[0] = jnp.dot(cs, w_ref[0], preferred_element_type=F32,
                       precision=lax.Precision.HIGHEST) + b_ref[0]


def _ada_mod(c, ada_w, ada_b):
    depth, d, n = ada_w.shape
    b = c.shape[0]
    tn = min(1024, n)
    return pl.pallas_call(
        _ada_kernel,
        grid=(depth, n // tn),
        in_specs=[pl.BlockSpec((b, d), lambda l, j: (0, 0)),
                  pl.BlockSpec((1, d, tn), lambda l, j: (l, 0, j)),
                  pl.BlockSpec((1, 1, tn), lambda l, j: (l, 0, j))],
        out_specs=pl.BlockSpec((1, b, tn), lambda l, j: (l, 0, j)),
        out_shape=jax.ShapeDtypeStruct((depth, b, n), F32),
        compiler_params=_params(2, VMEM_LIMIT_BYTES),
        name="ada_mod",
    )(c, ada_w, ada_b.reshape(depth, 1, n))


def _rms(x, g):
    ms = jnp.mean(x * x, axis=-1, keepdims=True)
    return x * lax.rsqrt(ms + EPS) * g


def _modnorm_kernel(x_ref, g_ref, mod_ref, h_ref, *, sh_row, sc_row):
    y = _rms(x_ref[...], g_ref[...])
    h = y * (1.0 + mod_ref[0, sc_row:sc_row + 1, :]) + mod_ref[0, sh_row:sh_row + 1, :]
    h_ref[...] = h.astype(h_ref.dtype)


def _modnorm(x2, g, mod, seq, sh_row, sc_row):
    t, d = x2.shape
    tm = min(512, seq)
    nsb = seq // tm
    return pl.pallas_call(
        functools.partial(_modnorm_kernel, sh_row=sh_row, sc_row=sc_row),
        grid=(t // tm,),
        in_specs=[pl.BlockSpec((tm, d), lambda i: (i, 0)),
                  pl.BlockSpec((1, d), lambda i: (0, 0)),
                  pl.BlockSpec((1, 6, d), lambda i: (i // nsb, 0, 0))],
        out_specs=pl.BlockSpec((tm, d), lambda i: (i, 0)),
        out_shape=jax.ShapeDtypeStruct((t, d), BF16),
        compiler_params=_params(1, VMEM_LIMIT_BYTES),
        name="modnorm",
    )(x2, g.reshape(1, d), mod)


def _rope128(a, cos, sin_signed):
    return a * cos + pltpu.roll(a, HEAD_DIM // 2, 1) * sin_signed


def _rope64(a, cos, sin_signed):
    lane = lax.broadcasted_iota(I32, a.shape, 1)
    first_half = (lane & (IDX_DIM - 1)) < (IDX_DIM // 2)
    rot = jnp.where(first_half, pltpu.roll(a, LANES - IDX_DIM // 2, 1), pltpu.roll(a, IDX_DIM // 2, 1))
    return a * cos + rot * sin_signed


def _mm_plain_kernel(h_ref, w_ref, o_ref):
    o_ref[...] = jnp.dot(h_ref[...], w_ref[...], preferred_element_type=F32).astype(o_ref.dtype)


def _mm_rope_kernel(h_ref, w_ref, cos_ref, sin_ref, o_ref, *, rope, post_scale):
    acc = jnp.dot(h_ref[...], w_ref[...], preferred_element_type=F32)
    cos = cos_ref[...] * post_scale
    sin = sin_ref[...] * post_scale
    for t in range(acc.shape[1] // LANES):
        sl = slice(t * LANES, (t + 1) * LANES)
        o_ref[:, sl] = rope(acc[:, sl], cos, sin).astype(o_ref.dtype)


def _mm_kvi_kernel(h_ref, w_ref, cos_ref, sin_ref, cosi_ref, sini_ref,
                   k_ref, v_ref, kia_ref, kib_ref, wi_ref):
    acc = jnp.dot(h_ref[...], w_ref[...], preferred_element_type=F32)
    k_ref[...] = _rope128(acc[:, :LANES], cos_ref[...], sin_ref[...]).astype(k_ref.dtype)
    v_ref[:, :LANES] = acc[:, LANES:2 * LANES].astype(v_ref.dtype)
    v_ref[:, LANES:] = jnp.ones((acc.shape[0], LANES), v_ref.dtype)
    t3 = acc[:, 2 * LANES:]
    lane = lax.broadcasted_iota(I32, t3.shape, 1)
    ki = jnp.where(lane < IDX_DIM, _rope64(t3, cosi_ref[...], sini_ref[...]), 0.0)
    kia_ref[...] = ki.astype(kia_ref.dtype)
    kib_ref[...] = pltpu.roll(ki, IDX_DIM, 1).astype(kib_ref.dtype)
    wi_ref[...] = pltpu.roll(t3, IDX_DIM, 1) * (IDX_HEADS ** -0.5 * IDX_DIM ** -0.5)


def _matmul(h, w, out_dtype, tn, tm=1024, name="matmul"):
    t, k = h.shape
    n = w.shape[1]
    tm = min(tm, t)
    tn = min(tn, n)
    return pl.pallas_call(
        _mm_plain_kernel,
        grid=(t // tm, n // tn),
        in_specs=[pl.BlockSpec((tm, k), lambda i, j: (i, 0)),
                  pl.BlockSpec((k, tn), lambda i, j: (0, j))],
        out_specs=pl.BlockSpec((tm, tn), lambda i, j: (i, j)),
        out_shape=jax.ShapeDtypeStruct((t, n), out_dtype),
        compiler_params=_params(2, VMEM_LIMIT_BYTES),
        name=name,
    )(h, w)


def _matmul_rope(h, w, cos, sin, rope, tn, tm=1024, post_scale=1.0, name="matmul_rope"):
    t, k = h.shape
    n = w.shape[1]
    tm = min(tm, t)
    tn = min(tn, n)
    tab = pl.BlockSpec((tm, LANES), lambda i, j: (i, 0))
    return pl.pallas_call(
        functools.partial(_mm_rope_kernel, rope=rope, post_scale=post_scale),
        grid=(t // tm, n // tn),
        in_specs=[pl.BlockSpec((tm, k), lambda i, j: (i, 0)),
                  pl.BlockSpec((k, tn), lambda i, j: (0, j)), tab, tab],
        out_specs=pl.BlockSpec((tm, tn), lambda i, j: (i, j)),
        out_shape=jax.ShapeDtypeStruct((t, n), BF16),
        compiler_params=_params(2, VMEM_LIMIT_BYTES),
        name=name,
    )(h, w, cos, sin)


def _matmul_kvi(h, w, cos, sin, cosi, sini, tm=1024):
    t, k = h.shape
    n = w.shape[1]
    tm = min(tm, t)
    tab = pl.BlockSpec((tm, LANES), lambda i: (i, 0))
    shp = lambda dt: jax.ShapeDtypeStruct((t, LANES), dt)
    return pl.pallas_call(
        _mm_kvi_kernel,
        grid=(t // tm,),
        in_specs=[pl.BlockSpec((tm, k), lambda i: (i, 0)),
                  pl.BlockSpec((k, n), lambda i: (0, 0)), tab, tab, tab, tab],
        out_specs=[tab, pl.BlockSpec((tm, 2 * LANES), lambda i: (i, 0)), tab, tab, tab],
        out_shape=[shp(BF16), jax.ShapeDtypeStruct((t, 2 * LANES), BF16), shp(BF16), shp(BF16), shp(F32)],
        compiler_params=_params(1, VMEM_LIMIT_BYTES),
        name="matmul_kvi",
    )(h, w, cos, sin, cosi, sini)


def _rope_tab_kernel(pos_ref, inv_ref, sgn_ref, cos_ref, sin_ref):
    ang = pos_ref[...].astype(F32) * inv_ref[...]
    cos_ref[...] = jnp.cos(ang)
    sin_ref[...] = jnp.sin(ang) * sgn_ref[...]


def _rope_tables(pos_col, half):
    t = pos_col.shape[0]
    inv = ROPE_THETA ** (-jnp.arange(half, dtype=F32) / half)
    reps = LANES // (2 * half)
    inv_row = jnp.tile(jnp.concatenate([inv, inv]), reps).reshape(1, LANES)
    sgn_row = jnp.tile(jnp.concatenate([-jnp.ones(half, F32), jnp.ones(half, F32)]), reps).reshape(1, LANES)
    tm = min(1024, t)
    row = pl.BlockSpec((1, LANES), lambda i: (0, 0))
    tab = pl.BlockSpec((tm, LANES), lambda i: (i, 0))
    return pl.pallas_call(
        _rope_tab_kernel,
        grid=(t // tm,),
        in_specs=[pl.BlockSpec((tm, 1), lambda i: (i, 0)), row, row],
        out_specs=[tab, tab],
        out_shape=[jax.ShapeDtypeStruct((t, LANES), F32)] * 2,
        compiler_params=_params(1),
        name="rope_tables",
    )(pos_col, inv_row, sgn_row)


def _gate_cumsum_kernel(fg_ref, b_ref, f_ref, carry_ref):
    @pl.when(pl.program_id(1) == 0)
    def _():
        carry_ref[...] = jnp.zeros_like(carry_ref)

    lf = _log_sigmoid(fg_ref[...] + b_ref[...])
    tc = lf.shape[0]
    r = lax.broadcasted_iota(I32, (tc, tc), 0)
    c = lax.broadcasted_iota(I32, (tc, tc), 1)
    tri = (c <= r).astype(BF16)
    cs = carry_ref[...]
    for piece in _split_bf16(lf, 3):
        cs = cs + jnp.dot(tri, piece, preferred_element_type=F32)
    f_ref[...] = cs * LOG2E
    carry_ref[...] = cs[tc - 1:tc, :]


def _gate_cumsum(fg, b_row, batch, seq):
    tc = min(256, seq)
    nsb = seq // tc
    return pl.pallas_call(
        _gate_cumsum_kernel,
        grid=(batch, nsb),
        in_specs=[pl.BlockSpec((tc, LANES), lambda b, j: (b * nsb + j, 0)),
                  pl.BlockSpec((1, LANES), lambda b, j: (0, 0))],
        out_specs=pl.BlockSpec((tc, LANES), lambda b, j: (b * nsb + j, 0)),
        out_shape=jax.ShapeDtypeStruct(fg.shape, F32),
        scratch_shapes=[pltpu.VMEM((1, LANES), F32)],
        compiler_params=_params(2),
        name="gate_cumsum",
    )(fg, b_row)


def _softmax_block(tiles, m_prev):
    mx = functools.reduce(jnp.maximum, tiles)
    m_new = jnp.maximum(m_prev, jnp.broadcast_to(jnp.max(mx, axis=-1, keepdims=True), mx.shape))
    alpha = jnp.exp2(m_prev - m_new)
    return m_new, alpha, [jnp.exp2(t - m_new) for t in tiles]


def _fox_kernel(q_ref, k_ref, v_ref, f_ref, o_ref, m_ref, l_ref, acc_ref, sa_ref, sb_ref, *, tq):
    i = pl.program_id(2)
    m_ref[...] = jnp.full_like(m_ref, -jnp.inf)
    l_ref[...] = jnp.zeros_like(l_ref)
    acc_ref[...] = jnp.zeros_like(acc_ref)
    n_tiles = tq // LANES

    def keys(kj):
        return pl.ds(pl.multiple_of(kj * tq, tq), tq)

    def scores(kj, s_ref):
        s_ref[...] = _dot_nt(q_ref[...], k_ref[keys(kj), :])

    def softmax_pv(kj, s_ref, masked):
        ks = keys(kj)
        s = s_ref[...] - f_ref[0, :, ks]
        if masked:
            row = lax.broadcasted_iota(I32, s.shape, 0)
            col = lax.broadcasted_iota(I32, s.shape, 1)
            s = jnp.where(col <= row, s, -jnp.inf)
        tiles = [s[:, t * LANES:(t + 1) * LANES] for t in range(n_tiles)]
        m_new, alpha, p = _softmax_block(tiles, m_ref[...])
        psum = functools.reduce(jnp.add, p)
        l_ref[...] = alpha * l_ref[...] + jnp.broadcast_to(jnp.sum(psum, axis=-1, keepdims=True), psum.shape)
        pv = jnp.dot(jnp.concatenate(p, axis=-1).astype(BF16), v_ref[ks, :], preferred_element_type=F32)
        acc_ref[...] = alpha * acc_ref[...] + pv
        m_ref[...] = m_new

    scores(0, sa_ref)

    def pair(p, carry):
        scores(2 * p + 1, sb_ref)
        softmax_pv(2 * p, sa_ref, masked=False)
        scores(2 * p + 2, sa_ref)
        softmax_pv(2 * p + 1, sb_ref, masked=False)
        return carry

    lax.fori_loop(0, i // 2, pair, 0)

    @pl.when(i % 2 == 0)
    def _():
        softmax_pv(i, sa_ref, masked=True)

    @pl.when(i % 2 == 1)
    def _():
        scores(i, sb_ref)
        softmax_pv(i - 1, sa_ref, masked=False)
        softmax_pv(i, sb_ref, masked=True)

    o_ref[...] = (acc_ref[...] / l_ref[...]).astype(o_ref.dtype)


def _fox_attention(qkv, f_rows, batch, seq, n_heads, q_col, k_col, v_col):
    tq = min(512, seq)
    nq = seq // tq
    t = qkv.shape[0]
    kernel = functools.partial(_fox_kernel, tq=tq)
    return pl.pallas_call(
        kernel,
        grid=(batch, n_heads, nq),
        in_specs=[pl.BlockSpec((tq, HEAD_DIM), lambda b, h, i: (b * nq + i, q_col + h)),
                  pl.BlockSpec((seq, HEAD_DIM), lambda b, h, i: (b, k_col + h)),
                  pl.BlockSpec((seq, HEAD_DIM), lambda b, h, i: (b, v_col + h)),
                  pl.BlockSpec((1, 1, seq), lambda b, h, i: (b * n_heads + h, 0, 0))],
        out_specs=pl.BlockSpec((tq, HEAD_DIM), lambda b, h, i: (b * nq + i, h)),
        out_shape=jax.ShapeDtypeStruct((t, n_heads * HEAD_DIM), BF16),
        scratch_shapes=[pltpu.VMEM((tq, LANES), F32), pltpu.VMEM((tq, LANES), F32),
                        pltpu.VMEM((tq, HEAD_DIM), F32),
                        pltpu.VMEM((tq, tq), F32), pltpu.VMEM((tq, tq), F32)],
        compiler_params=_params(3, VMEM_LIMIT_BYTES),
        name="fox_attention",
    )(qkv, qkv, qkv, f_rows)


def _sb_kernel(q_ref, k_ref, v_ref, o_ref, c_ref, acc_ref, *, tq, tk):
    i = pl.program_id(2)
    q = q_ref[...]
    c_ref[...] = jnp.zeros_like(c_ref)
    acc_ref[...] = jnp.zeros_like(acc_ref)
    r = lax.broadcasted_iota(I32, (2 * tk, tk), 0)
    cc = lax.broadcasted_iota(I32, (2 * tk, tk), 1)
    upper2 = ((r & (tk - 1)) > cc).astype(BF16)
    nblk = (i + 1) * (tq // tk)

    def cond(carry):
        step, c_max = carry
        return jnp.logical_and(step < nblk, c_max > SB_EXIT_LOG)

    def body(carry):
        step, _ = carry
        kj = nblk - 1 - step
        ks = pl.ds(pl.multiple_of(kj * tk, tk), tk)
        z = _dot_nt(q, k_ref[ks, :])
        row = lax.broadcasted_iota(I32, z.shape, 0) + i * tq
        col = lax.broadcasted_iota(I32, z.shape, 1) + kj * tk
        strict = col < row
        ls = _log_sigmoid(z)
        lneg = jnp.where(strict, ls - z, 0.0)
        suffix = jnp.dot(jnp.concatenate(_split_bf16(lneg, 2), axis=1), upper2, preferred_element_type=F32)
        c_prev = c_ref[...]
        a = jnp.where(strict, jnp.exp(ls + suffix + c_prev), 0.0)
        acc_ref[...] += jnp.dot(a.astype(BF16), v_ref[ks, :], preferred_element_type=F32)
        c_new = c_prev + suffix[:, 0:1] + lneg[:, 0:1]
        c_ref[...] = c_new
        return step + 1, jnp.max(c_new)

    lax.while_loop(cond, body, (jnp.int32(0), jnp.float32(0.0)))
    o_ref[...] = acc_ref[...].astype(o_ref.dtype)


def _sb_attention(qkv, batch, seq, n_heads, q_col, k_col, v_col):
    tq = min(512, seq)
    tk = min(256, seq)
    nq = seq // tq
    t = qkv.shape[0]
    kernel = functools.partial(_sb_kernel, tq=tq, tk=tk)
    return pl.pallas_call(
        kernel,
        grid=(batch, n_heads, nq),
        in_specs=[pl.BlockSpec((tq, HEAD_DIM), lambda b, h, i: (b * nq + i, q_col + h)),
                  pl.BlockSpec((seq, HEAD_DIM), lambda b, h, i: (b, k_col + h)),
                  pl.BlockSpec((seq, HEAD_DIM), lambda b, h, i: (b, v_col + h))],
        out_specs=pl.BlockSpec((tq, HEAD_DIM), lambda b, h, i: (b * nq + i, h)),
        out_shape=jax.ShapeDtypeStruct((t, n_heads * HEAD_DIM), BF16),
        scratch_shapes=[pltpu.VMEM((tq, 1), F32), pltpu.VMEM((tq, HEAD_DIM), F32)],
        compiler_params=_params(3, VMEM_LIMIT_BYTES),
        name="sb_attention",
    )(qkv, qkv, qkv)


def _dsa_kernel(q_ref, qi_ref, wi_ref, k_ref, v_ref, kia_ref, kib_ref, o_ref,
                keys_ref, keyst_ref, qs_ref, m_ref, acc_ref, sa_ref, sb_ref, *, n_heads, kc, kca, topk):
    i = pl.program_id(1)
    tq = Q_BLOCK
    nch = ((i + 1) * tq + kc - 1) // kc
    row_t = lax.broadcasted_iota(I32, (tq, kc), 0) + i * tq
    col_l = lax.broadcasted_iota(I32, (tq, kc), 1)

    def index_body(c, carry):
        ks = pl.ds(pl.multiple_of(c * kc, kc), kc)
        kia = kia_ref[ks, :]
        kib = kib_ref[ks, :]
        wi = wi_ref[...]
        score = jnp.zeros((tq, kc), F32)
        for p in range(IDX_HEADS // 2):
            qp = qi_ref[:, p * LANES:(p + 1) * LANES]
            score = score + jnp.maximum(_dot_nt(qp, kia), 0.0) * wi[:, 2 * p:2 * p + 1]
            score = score + jnp.maximum(_dot_nt(qp, kib), 0.0) * wi[:, 2 * p + 1:2 * p + 2]
        bits = lax.bitcast_convert_type(score + 0.0, I32)
        key = bits ^ ((bits >> 31) & 0x7FFFFFFF)
        adm = ((col_l + c * kc) >> CHUNK_SHIFT) <= (row_t >> CHUNK_SHIFT)
        key = jnp.where(adm, key, INT_MIN)
        keys_ref[:, ks] = key
        keyst_ref[ks, :] = key.T
        return carry

    lax.fori_loop(0, nch, index_body, 0)

    def count_ge(trial):
        def body(c, cnt):
            kk = keyst_ref[pl.ds(pl.multiple_of(c * kc, kc), kc), :]
            hit = jnp.where(kk >= trial, 1.0, 0.0)
            return cnt + jnp.sum(hit.reshape(kc // COUNT_ROWS, COUNT_ROWS, tq), axis=0)
        cnt = lax.fori_loop(0, nch, body, jnp.zeros((COUNT_ROWS, tq), F32))
        return jnp.sum(cnt, axis=0, keepdims=True)

    def per_row(v):
        return jnp.broadcast_to(v, (LANES, tq)).T

    kf = float(topk)
    cur_q = jnp.where(count_ge(jnp.zeros((1, tq), I32)) >= kf, 0, INT_MIN).astype(I32)

    def search_body(it, cur):
        trial = cur + jnp.left_shift(jnp.int32(1), 30 - it)
        return jnp.where(count_ge(trial) >= kf, trial, cur)

    cur_q = lax.fori_loop(0, 31, search_body, cur_q)
    cur = per_row(cur_q)
    thr = jnp.maximum(cur, INT_MIN + 1)

    surplus = jnp.where((count_ge(cur_q) > kf) & (cur_q > INT_MIN), 1.0, 0.0)

    @pl.when(jnp.max(surplus) > 0.0)
    def _():
        need = per_row(kf - count_ge(cur_q + 1))
        r = lax.broadcasted_iota(I32, (kc, kc), 0)
        cc = lax.broadcasted_iota(I32, (kc, kc), 1)
        before = (r < cc).astype(BF16)
        lane_tiles = [slice(t * LANES, (t + 1) * LANES) for t in range(kc // LANES)]

        def tie_body(c, run):
            base = pl.multiple_of(c * kc, kc)
            kk = keys_ref[:, pl.ds(base, kc)]
            eq = [kk[:, sl] == cur for sl in lane_tiles]
            eqf = [jnp.where(e, 1.0, 0.0) for e in eq]
            rank = jnp.dot(jnp.concatenate(eqf, axis=-1).astype(BF16), before, preferred_element_type=F32)
            for t, sl in enumerate(lane_tiles):
                retire = eq[t] & (rank[:, sl] + run >= need)
                keys_ref[:, pl.ds(base + t * LANES, LANES)] = jnp.where(retire, INT_MIN, kk[:, sl])
            tot = functools.reduce(jnp.add, eqf)
            return run + jnp.broadcast_to(jnp.sum(tot, axis=-1, keepdims=True), tot.shape)

        lax.fori_loop(0, nch, tie_body, jnp.zeros((tq, LANES), F32))

    for h in range(n_heads):
        qs_ref[h * tq:(h + 1) * tq, :] = q_ref[:, h * HEAD_DIM:(h + 1) * HEAD_DIM]
    m_ref[...] = jnp.full_like(m_ref, MASK_BIAS)
    acc_ref[...] = jnp.zeros_like(acc_ref)
    n_att = ((i + 1) * tq + kca - 1) // kca
    n_tiles = kca // LANES
    scored = nch * kc

    def chunk(c):
        return pl.ds(pl.multiple_of(c * kca, kca), kca)

    def scores(c, s_ref):
        s_ref[...] = _dot_nt(qs_ref[...], k_ref[chunk(jnp.minimum(c, n_att - 1)), :])

    def softmax_pv(c, s_ref):
        ks = chunk(c)
        lane = lax.broadcasted_iota(I32, (tq, LANES), 1)
        kk = keys_ref[:, ks]
        s = s_ref[...].reshape(n_heads, tq, kca)
        tiles = []
        for t in range(n_tiles):
            sl = slice(t * LANES, (t + 1) * LANES)
            bias = jnp.where((kk[:, sl] >= thr) & (lane < scored - c * kca - t * LANES), 0.0, MASK_BIAS)
            tiles.append(s[:, :, sl] + bias[None])
        m_new, alpha, p = _softmax_block(tiles, m_ref[...])
        pb = jnp.concatenate(p, axis=-1).reshape(n_heads * tq, kca).astype(BF16)
        pv = jnp.dot(pb, v_ref[ks, :], preferred_element_type=F32)
        a2 = alpha.reshape(n_heads * tq, LANES)
        acc_ref[...] = jnp.concatenate([a2, a2], axis=-1) * acc_ref[...] + pv
        m_ref[...] = m_new

    scores(0, sa_ref)

    def pair(p, carry):
        scores(2 * p + 1, sb_ref)
        softmax_pv(2 * p, sa_ref)
        scores(2 * p + 2, sa_ref)
        softmax_pv(2 * p + 1, sb_ref)
        return carry

    lax.fori_loop(0, n_att // 2, pair, 0)

    @pl.when(n_att % 2 == 1)
    def _():
        softmax_pv(n_att - 1, sa_ref)

    acc = acc_ref[...]
    out = acc[:, :HEAD_DIM] / acc[:, HEAD_DIM:]
    for h in range(n_heads):
        o_ref[:, h * HEAD_DIM:(h + 1) * HEAD_DIM] = out[h * tq:(h + 1) * tq, :].astype(o_ref.dtype)


def _dsa_attention(q, qi, wi, k, v, kia, kib, batch, seq, n_heads):
    tq = Q_BLOCK
    kc = min(256, seq)
    nq = seq // tq
    topk = min(TOPK_MAX, seq // 4)
    t = q.shape[0]
    kca = min(512, seq)
    kernel = functools.partial(_dsa_kernel, n_heads=n_heads, kc=kc, kca=kca, topk=topk)
    qblk = lambda w: pl.BlockSpec((tq, w), lambda b, i: (b * nq + i, 0))
    full = pl.BlockSpec((seq, LANES), lambda b, i: (b, 0))
    vext = pl.BlockSpec((seq, 2 * LANES), lambda b, i: (b, 0))
    return pl.pallas_call(
        kernel,
        grid=(batch, nq),
        in_specs=[qblk(n_heads * HEAD_DIM), qblk(IDX_HEADS * IDX_DIM), qblk(LANES), full, vext, full, full],
        out_specs=qblk(n_heads * HEAD_DIM),
        out_shape=jax.ShapeDtypeStruct((t, n_heads * HEAD_DIM), BF16),
        scratch_shapes=[pltpu.VMEM((tq, seq), I32),
                        pltpu.VMEM((seq, tq), I32),
                        pltpu.VMEM((n_heads * tq, HEAD_DIM), BF16),
                        pltpu.VMEM((n_heads, tq, LANES), F32),
                        pltpu.VMEM((n_heads * tq, 2 * LANES), F32),
                        pltpu.VMEM((n_heads * tq, kca), F32),
                        pltpu.VMEM((n_heads * tq, kca), F32)],
        compiler_params=_params(2, VMEM_LIMIT_BYTES),
        name="dsa_attention",
    )(q, qi, wi, k, v, kia, kib)


def _outproj_kernel(*refs, n_parts, gate_row):
    o_refs = refs[:n_parts]
    w_refs = refs[n_parts:2 * n_parts]
    x_ref, g_ref, mod_ref, out_ref = refs[2 * n_parts:]
    y = jnp.dot(o_refs[0][...], w_refs[0][...], preferred_element_type=F32)
    for o_r, w_r in zip(o_refs[1:], w_refs[1:]):
        y = y + jnp.dot(o_r[...], w_r[...], preferred_element_type=F32)
    out_ref[...] = x_ref[...] + mod_ref[0, gate_row:gate_row + 1, :] * _rms(y, g_ref[...])


def _outproj_residual(o_parts, w_parts, x2, g, mod, seq, gate_row):
    t, d = x2.shape
    tm = min(512, seq)
    nsb = seq // tm
    n_parts = len(o_parts)
    in_specs = [pl.BlockSpec((tm, o.shape[1]), lambda i: (i, 0)) for o in o_parts]
    in_specs += [pl.BlockSpec(w.shape, lambda i: (0, 0)) for w in w_parts]
    in_specs += [pl.BlockSpec((tm, d), lambda i: (i, 0)),
                 pl.BlockSpec((1, d), lambda i: (0, 0)),
                 pl.BlockSpec((1, 6, d), lambda i: (i // nsb, 0, 0))]
    return pl.pallas_call(
        functools.partial(_outproj_kernel, n_parts=n_parts, gate_row=gate_row),
        grid=(t // tm,),
        in_specs=in_specs,
        out_specs=pl.BlockSpec((tm, d), lambda i: (i, 0)),
        out_shape=jax.ShapeDtypeStruct((t, d), F32),
        compiler_params=_params(1, VMEM_LIMIT_BYTES),
        name="outproj_residual",
    )(*o_parts, *w_parts, x2, g.reshape(1, d), mod)


def _ffn_kernel(x_ref, g_in_ref, g_out_ref, mod_ref, w1_ref, w2_ref, out_ref, h_ref, acc_ref):
    j = pl.program_id(1)

    @pl.when(j == 0)
    def _():
        y = _rms(x_ref[...], g_in_ref[...])
        h_ref[...] = (y * (1.0 + mod_ref[0, 4:5, :]) + mod_ref[0, 3:4, :]).astype(h_ref.dtype)
        acc_ref[...] = jnp.zeros_like(acc_ref)

    u = jnp.maximum(jnp.dot(h_ref[...], w1_ref[...], preferred_element_type=F32), 0.0)
    acc_ref[...] += jnp.dot((u * u).astype(BF16), w2_ref[...], preferred_element_type=F32)

    @pl.when(j == pl.num_programs(1) - 1)
    def _():
        out_ref[...] = x_ref[...] + mod_ref[0, 5:6, :] * _rms(acc_ref[...], g_out_ref[...])


def _ffn_residual(x2, g_in, g_out, mod, w1, w2, seq):
    t, d = x2.shape
    f = w1.shape[1]
    tm = min(512, seq)
    tf = min(1024, f)
    nsb = seq // tm
    row = pl.BlockSpec((1, d), lambda i, j: (0, 0))
    return pl.pallas_call(
        _ffn_kernel,
        grid=(t // tm, f // tf),
        in_specs=[pl.BlockSpec((tm, d), lambda i, j: (i, 0)), row, row,
                  pl.BlockSpec((1, 6, d), lambda i, j: (i // nsb, 0, 0)),
                  pl.BlockSpec((d, tf), lambda i, j: (0, j)),
                  pl.BlockSpec((tf, d), lambda i, j: (j, 0))],
        out_specs=pl.BlockSpec((tm, d), lambda i, j: (i, 0)),
        out_shape=jax.ShapeDtypeStruct((t, d), F32),
        scratch_shapes=[pltpu.VMEM((tm, d), BF16), pltpu.VMEM((tm, d), F32)],
        compiler_params=_params(2, VMEM_LIMIT_BYTES),
        name="ffn_residual",
    )(x2, g_in.reshape(1, d), g_out.reshape(1, d), mod, w1, w2)


def _even_layer(x2, mod, norm_g, w_in, b_forget, w_out, batch, seq):
    d = x2.shape[1]
    n_heads = d // HEAD_DIM
    n_fox = n_heads // 2
    n_sb = n_heads - n_fox
    fw = n_fox * HEAD_DIM
    sw = n_sb * HEAD_DIM
    scale = HEAD_DIM ** -0.5
    w_main = jnp.concatenate([w_in[:, :fw] * (scale * LOG2E), w_in[:, fw:3 * fw],
                              w_in[:, 3 * fw + n_fox:3 * fw + n_fox + sw] * scale,
                              w_in[:, 3 * fw + n_fox + sw:]], axis=1).astype(BF16)
    w_gate = jnp.pad(w_in[:, 3 * fw:3 * fw + n_fox], ((0, 0), (0, LANES - n_fox))).astype(BF16)
    b_row = jnp.pad(b_forget.astype(F32), (0, LANES - n_fox)).reshape(1, LANES)

    h = _modnorm(x2, norm_g[0], mod, seq, sh_row=0, sc_row=1)
    qkv = _matmul(h, w_main, BF16, tn=512, name="even_in_proj")
    fg = _matmul(h, w_gate, F32, tn=LANES, name="even_gate_proj")
    f_cum = _gate_cumsum(fg, b_row, batch, seq)
    f_rows = f_cum.reshape(batch, seq, LANES)[:, :, :n_fox].transpose(0, 2, 1).reshape(batch * n_fox, 1, seq)
    o_f = _fox_attention(qkv, f_rows, batch, seq, n_fox, 0, n_fox, 2 * n_fox)
    o_s = _sb_attention(qkv, batch, seq, n_sb, 3 * n_fox, 3 * n_fox + n_sb, 3 * n_fox + 2 * n_sb)
    w_o = w_out.astype(BF16)
    return _outproj_residual([o_f, o_s], [w_o[:fw], w_o[fw:]], x2, norm_g[1], mod, seq, gate_row=2)


def _odd_layer(x2, mod, norm_g, w_in, w_out, pos_col, batch, seq):
    d = x2.shape[1]
    n_heads = d // HEAD_DIM
    qw = n_heads * HEAD_DIM
    iw = IDX_HEADS * IDX_DIM
    o_k, o_v, o_qi, o_ki, o_wi = qw, qw + HEAD_DIM, qw + 2 * HEAD_DIM, qw + 2 * HEAD_DIM + iw, qw + 2 * HEAD_DIM + iw + IDX_DIM
    w_q = w_in[:, :qw].astype(BF16)
    w_qi = w_in[:, o_qi:o_ki].astype(BF16)
    w_kvi = jnp.concatenate([w_in[:, o_k:o_qi], w_in[:, o_ki:],
                             jnp.zeros((d, LANES - IDX_DIM - IDX_HEADS), w_in.dtype)], axis=1).astype(BF16)

    cos, sin = _rope_tables(pos_col, HEAD_DIM // 2)
    cosi, sini = _rope_tables(pos_col, IDX_DIM // 2)
    h = _modnorm(x2, norm_g[0], mod, seq, sh_row=0, sc_row=1)
    q = _matmul_rope(h, w_q, cos, sin, _rope128, tn=512, post_scale=HEAD_DIM ** -0.5 * LOG2E, name="odd_q_proj")
    qi = _matmul_rope(h, w_qi, cosi, sini, _rope64, tn=512, name="odd_qi_proj")
    k, v, kia, kib, wi = _matmul_kvi(h, w_kvi, cos, sin, cosi, sini)
    o = _dsa_attention(q, qi, wi, k, v, kia, kib, batch, seq, n_heads)
    return _outproj_residual([o], [w_out.astype(BF16)], x2, norm_g[1], mod, seq, gate_row=2)


def kernel(x, c, positions, ada_w, ada_b, norm_g, mix_w_out, even_w_in, even_b_forget, odd_w_in, ff_w1, ff_w2):
    batch, seq, d = x.shape
    depth = ada_w.shape[0]
    assert d % HEAD_DIM == 0 and seq % Q_BLOCK == 0 and seq >= TOPK_MAX
    mods = _ada_mod(c, ada_w, ada_b).reshape(depth, batch, 6, d)
    pos_col = positions.reshape(batch * seq, 1)
    x2 = x.reshape(batch * seq, d)
    for l in range(depth):
        mod = mods[l]
        if l % 2 == 0:
            x2 = _even_layer(x2, mod, norm_g[l], even_w_in[l // 2], even_b_forget[l // 2],
                             mix_w_out[l], batch, seq)
        else:
            x2 = _odd_layer(x2, mod, norm_g[l], odd_w_in[l // 2], mix_w_out[l], pos_col, batch, seq)
        x2 = _ffn_residual(x2, norm_g[l, 2], norm_g[l, 3], mod,
                           ff_w1[l].astype(BF16), ff_w2[l].astype(BF16), seq)
    return x2.reshape(batch, seq, d)
```

```python
import functools

import jax
import jax.numpy as jnp
from jax import lax
from jax.experimental import pallas as pl
from jax.experimental.pallas import tpu as pltpu

F32 = jnp.float32
BF16 = jnp.bfloat16
I32 = jnp.int32

HEAD_DIM = 128
CHUNK = 64
CHUNK_SHIFT = 6
Q_BLOCK = 128
IDX_HEADS = 16
IDX_DIM = 64
TOPK_MAX = 256
ROPE_THETA = 10000.0
EPS = 1e-6

LANES = 128
SUBLANES = 8
COUNT_ROWS = 4 * SUBLANES
INT_MIN = -(2 ** 31)
MASK_BIAS = -1e30
SB_EXIT_LOG2 = -152.0
LOG2E = 1.4426950408889634
VMEM_LIMIT_BYTES = 56 * 1024 * 1024


def _params(n_axes, vmem=None):
    kw = dict(dimension_semantics=("arbitrary",) * n_axes)
    if vmem is not None:
        kw["vmem_limit_bytes"] = vmem
    return pltpu.CompilerParams(**kw)


def _dot_nt(a, b):
    return lax.dot_general(a, b, (((1,), (1,)), ((), ())), preferred_element_type=F32)


def _split_bf16(x, parts):
    out = []
    r = x
    for _ in range(parts):
        p = r.astype(BF16)
        out.append(p)
        r = r - p.astype(F32)
    return out


def _log_sigmoid(x):
    return jnp.minimum(x, 0.0) - jnp.log1p(jnp.exp(-jnp.abs(x)))


def _ada_kernel(c_ref, w_ref, b_ref, o_ref):
    c = c_ref[...]
    cs = c / (1.0 + jnp.exp(-c))
    o_ref[0] = jnp.dot(cs, w_ref[0], preferred_element_type=F32,
                       precision=lax.Precision.HIGHEST) + b_ref[0]


def _ada_mod(c, ada_w, ada_b):
    depth, d, n = ada_w.shape
    b = c.shape[0]
    tn = min(1024, n)
    return pl.pallas_call(
        _ada_kernel,
        grid=(depth, n // tn),
        in_specs=[pl.BlockSpec((b, d), lambda l, j: (0, 0)),
                  pl.BlockSpec((1, d, tn), lambda l, j: (l, 0, j)),
                  pl.BlockSpec((1, 1, tn), lambda l, j: (l, 0, j))],
        out_specs=pl.BlockSpec((1, b, tn), lambda l, j: (l, 0, j)),
        out_shape=jax.ShapeDtypeStruct((depth, b, n), F32),
        compiler_params=_params(2, VMEM_LIMIT_BYTES),
        name="ada_mod",
    )(c, ada_w, ada_b.reshape(depth, 1, n))


def _rms(x, g):
    ms = jnp.mean(x * x, axis=-1, keepdims=True)
    return x * lax.rsqrt(ms + EPS) * g


def _modulated_norm(x, g, mod_ref, sh_row, sc_row):
    return _rms(x, g) * (1.0 + mod_ref[0, sc_row:sc_row + 1, :]) + mod_ref[0, sh_row:sh_row + 1, :]


def _rope128(a, cos, sin_signed):
    return a * cos + pltpu.roll(a, HEAD_DIM // 2, 1) * sin_signed


def _rope64(a, cos, sin_signed):
    lane = lax.broadcasted_iota(I32, a.shape, 1)
    first_half = (lane & (IDX_DIM - 1)) < (IDX_DIM // 2)
    rot = jnp.where(first_half, pltpu.roll(a, LANES - IDX_DIM // 2, 1), pltpu.roll(a, IDX_DIM // 2, 1))
    return a * cos + rot * sin_signed


def _even_proj_kernel(x_ref, g_ref, mod_ref, w_ref, wg_ref, o_ref, fg_ref, h_ref):
    @pl.when(pl.program_id(1) == 0)
    def _():
        h = _modulated_norm(x_ref[...], g_ref[...], mod_ref, 0, 1).astype(h_ref.dtype)
        h_ref[...] = h
        fg_ref[...] = jnp.dot(h, wg_ref[...], preferred_element_type=F32)

    o_ref[...] = jnp.dot(h_ref[...], w_ref[...], preferred_element_type=F32).astype(o_ref.dtype)


def _even_projection(x2, g, mod, w_main, w_gate, seq, tm=1024, tn=512):
    t, d = x2.shape
    n = w_main.shape[1]
    tm = min(tm, seq)
    tn = min(tn, n)
    nsb = seq // tm
    return pl.pallas_call(
        _even_proj_kernel,
        grid=(t // tm, n // tn),
        in_specs=[pl.BlockSpec((tm, d), lambda i, j: (i, 0)),
                  pl.BlockSpec((1, d), lambda i, j: (0, 0)),
                  pl.BlockSpec((1, 6, d), lambda i, j: (i // nsb, 0, 0)),
                  pl.BlockSpec((d, tn), lambda i, j: (0, j)),
                  pl.BlockSpec((d, LANES), lambda i, j: (0, 0))],
        out_specs=[pl.BlockSpec((tm, tn), lambda i, j: (i, j)),
                   pl.BlockSpec((tm, LANES), lambda i, j: (i, 0))],
        out_shape=[jax.ShapeDtypeStruct((t, n), BF16), jax.ShapeDtypeStruct((t, LANES), F32)],
        scratch_shapes=[pltpu.VMEM((tm, d), BF16)],
        compiler_params=_params(2, VMEM_LIMIT_BYTES),
        name="even_in_proj",
    )(x2, g.reshape(1, d), mod, w_main, w_gate)


def _odd_proj_kernel(x_ref, g_ref, mod_ref, w_ref, wkvi_ref, cos_ref, sin_ref, cosi_ref, sini_ref,
                     q_ref, qi_ref, k_ref, v_ref, kia_ref, kib_ref, wi_ref, h_ref, *, n_q_tiles, q_scale):
    j = pl.program_id(1)

    @pl.when(j == 0)
    def _():
        h = _modulated_norm(x_ref[...], g_ref[...], mod_ref, 0, 1).astype(h_ref.dtype)
        h_ref[...] = h
        acc = jnp.dot(h, wkvi_ref[...], preferred_element_type=F32)
        k_ref[...] = _rope128(acc[:, :LANES], cos_ref[...], sin_ref[...]).astype(k_ref.dtype)
        v_ref[:, :LANES] = acc[:, LANES:2 * LANES].astype(v_ref.dtype)
        v_ref[:, LANES:] = jnp.ones((acc.shape[0], LANES), v_ref.dtype)
        t3 = acc[:, 2 * LANES:]
        lane = lax.broadcasted_iota(I32, t3.shape, 1)
        ki = jnp.where(lane < IDX_DIM, _rope64(t3, cosi_ref[...], sini_ref[...]), 0.0)
        kia_ref[...] = ki.astype(kia_ref.dtype)
        kib_ref[...] = pltpu.roll(ki, IDX_DIM, 1).astype(kib_ref.dtype)
        wi_ref[...] = pltpu.roll(t3, IDX_DIM, 1) * (IDX_HEADS ** -0.5 * IDX_DIM ** -0.5)

    acc = jnp.dot(h_ref[...], w_ref[...], preferred_element_type=F32)
    lane_tiles = [slice(t * LANES, (t + 1) * LANES) for t in range(acc.shape[1] // LANES)]

    @pl.when(j < n_q_tiles)
    def _():
        cos = cos_ref[...] * q_scale
        sin = sin_ref[...] * q_scale
        for sl in lane_tiles:
            q_ref[:, sl] = _rope128(acc[:, sl], cos, sin).astype(q_ref.dtype)

    @pl.when(j >= n_q_tiles)
    def _():
        for sl in lane_tiles:
            qi_ref[:, sl] = _rope64(acc[:, sl], cosi_ref[...], sini_ref[...]).astype(qi_ref.dtype)


def _odd_projection(x2, g, mod, w_qqi, w_kvi, cos, sin, cosi, sini, seq, q_width, q_scale, tm=1024, tn=512):
    t, d = x2.shape
    n = w_qqi.shape[1]
    tm = min(tm, seq)
    tn = min(tn, q_width)
    nsb = seq // tm
    n_q_tiles = q_width // tn
    tab = pl.BlockSpec((tm, LANES), lambda i, j: (i, 0))
    shp = lambda w, dt: jax.ShapeDtypeStruct((t, w), dt)
    return pl.pallas_call(
        functools.partial(_odd_proj_kernel, n_q_tiles=n_q_tiles, q_scale=q_scale),
        grid=(t // tm, n // tn),
        in_specs=[pl.BlockSpec((tm, d), lambda i, j: (i, 0)),
                  pl.BlockSpec((1, d), lambda i, j: (0, 0)),
                  pl.BlockSpec((1, 6, d), lambda i, j: (i // nsb, 0, 0)),
                  pl.BlockSpec((d, tn), lambda i, j: (0, j)),
                  pl.BlockSpec(w_kvi.shape, lambda i, j: (0, 0)), tab, tab, tab, tab],
        out_specs=[pl.BlockSpec((tm, tn), lambda i, j: (i, jnp.minimum(j, n_q_tiles - 1))),
                   pl.BlockSpec((tm, tn), lambda i, j: (i, jnp.maximum(j - n_q_tiles, 0))),
                   tab, pl.BlockSpec((tm, 2 * LANES), lambda i, j: (i, 0)), tab, tab, tab],
        out_shape=[shp(q_width, BF16), shp(n - q_width, BF16), shp(LANES, BF16), shp(2 * LANES, BF16),
                   shp(LANES, BF16), shp(LANES, BF16), shp(LANES, F32)],
        scratch_shapes=[pltpu.VMEM((tm, d), BF16)],
        compiler_params=_params(2, VMEM_LIMIT_BYTES),
        name="odd_in_proj",
    )(x2, g.reshape(1, d), mod, w_qqi, w_kvi, cos, sin, cosi, sini)


def _rope_tab_kernel(pos_ref, inv_ref, sgn_ref, cos_ref, sin_ref):
    ang = pos_ref[...].astype(F32) * inv_ref[...]
    cos_ref[...] = jnp.cos(ang)
    sin_ref[...] = jnp.sin(ang) * sgn_ref[...]


def _rope_tables(pos_col, half):
    t = pos_col.shape[0]
    inv = ROPE_THETA ** (-jnp.arange(half, dtype=F32) / half)
    reps = LANES // (2 * half)
    inv_row = jnp.tile(jnp.concatenate([inv, inv]), reps).reshape(1, LANES)
    sgn_row = jnp.tile(jnp.concatenate([-jnp.ones(half, F32), jnp.ones(half, F32)]), reps).reshape(1, LANES)
    tm = min(1024, t)
    row = pl.BlockSpec((1, LANES), lambda i: (0, 0))
    tab = pl.BlockSpec((tm, LANES), lambda i: (i, 0))
    return pl.pallas_call(
        _rope_tab_kernel,
        grid=(t // tm,),
        in_specs=[pl.BlockSpec((tm, 1), lambda i: (i, 0)), row, row],
        out_specs=[tab, tab],
        out_shape=[jax.ShapeDtypeStruct((t, LANES), F32)] * 2,
        compiler_params=_params(1),
        name="rope_tables",
    )(pos_col, inv_row, sgn_row)


def _gate_cumsum_kernel(fg_ref, b_ref, f_ref, carry_ref):
    @pl.when(pl.program_id(1) == 0)
    def _():
        carry_ref[...] = jnp.zeros_like(carry_ref)

    lf = _log_sigmoid(fg_ref[...] + b_ref[...])
    tc = lf.shape[0]
    r = lax.broadcasted_iota(I32, (tc, tc), 0)
    c = lax.broadcasted_iota(I32, (tc, tc), 1)
    tri = (c <= r).astype(BF16)
    cs = carry_ref[...]
    for piece in _split_bf16(lf, 3):
        cs = cs + jnp.dot(tri, piece, preferred_element_type=F32)
    f_ref[...] = cs * LOG2E
    carry_ref[...] = cs[tc - 1:tc, :]


def _gate_cumsum(fg, b_row, batch, seq):
    tc = min(256, seq)
    nsb = seq // tc
    return pl.pallas_call(
        _gate_cumsum_kernel,
        grid=(batch, nsb),
        in_specs=[pl.BlockSpec((tc, LANES), lambda b, j: (b * nsb + j, 0)),
                  pl.BlockSpec((1, LANES), lambda b, j: (0, 0))],
        out_specs=pl.BlockSpec((tc, LANES), lambda b, j: (b * nsb + j, 0)),
        out_shape=jax.ShapeDtypeStruct(fg.shape, F32),
        scratch_shapes=[pltpu.VMEM((1, LANES), F32)],
        compiler_params=_params(2),
        name="gate_cumsum",
    )(fg, b_row)


def _softmax_block(tiles, m_prev):
    mx = functools.reduce(jnp.maximum, tiles)
    m_new = jnp.maximum(m_prev, jnp.broadcast_to(jnp.max(mx, axis=-1, keepdims=True), mx.shape))
    alpha = jnp.exp2(m_prev - m_new)
    return m_new, alpha, [jnp.exp2(t - m_new) for t in tiles]


def _fox_kernel(q_ref, k_ref, v_ref, f_ref, o_ref, m_ref, l_ref, acc_ref, sa_ref, sb_ref, *, tq):
    i = pl.program_id(2)
    m_ref[...] = jnp.full_like(m_ref, -jnp.inf)
    l_ref[...] = jnp.zeros_like(l_ref)
    acc_ref[...] = jnp.zeros_like(acc_ref)
    n_tiles = tq // LANES

    def keys(kj):
        return pl.ds(pl.multiple_of(kj * tq, tq), tq)

    def scores(kj, s_ref):
        s_ref[...] = _dot_nt(q_ref[...], k_ref[keys(kj), :])

    def softmax_pv(kj, s_ref, masked):
        ks = keys(kj)
        s = s_ref[...] - f_ref[0, :, ks]
        if masked:
            row = lax.broadcasted_iota(I32, s.shape, 0)
            col = lax.broadcasted_iota(I32, s.shape, 1)
            s = jnp.where(col <= row, s, -jnp.inf)
        tiles = [s[:, t * LANES:(t + 1) * LANES] for t in range(n_tiles)]
        m_new, alpha, p = _softmax_block(tiles, m_ref[...])
        psum = functools.reduce(jnp.add, p)
        l_ref[...] = alpha * l_ref[...] + jnp.broadcast_to(jnp.sum(psum, axis=-1, keepdims=True), psum.shape)
        pv = jnp.dot(jnp.concatenate(p, axis=-1).astype(BF16), v_ref[ks, :], preferred_element_type=F32)
        acc_ref[...] = alpha * acc_ref[...] + pv
        m_ref[...] = m_new

    scores(0, sa_ref)

    def pair(p, carry):
        scores(2 * p + 1, sb_ref)
        softmax_pv(2 * p, sa_ref, masked=False)
        scores(2 * p + 2, sa_ref)
        softmax_pv(2 * p + 1, sb_ref, masked=False)
        return carry

    lax.fori_loop(0, i // 2, pair, 0)

    @pl.when(i % 2 == 0)
    def _():
        softmax_pv(i, sa_ref, masked=True)

    @pl.when(i % 2 == 1)
    def _():
        scores(i, sb_ref)
        softmax_pv(i - 1, sa_ref, masked=False)
        softmax_pv(i, sb_ref, masked=True)

    o_ref[...] = (acc_ref[...] / l_ref[...]).astype(o_ref.dtype)


def _fox_attention(qkv, f_rows, batch, seq, n_heads, q_col, k_col, v_col):
    tq = min(512, seq)
    nq = seq // tq
    t = qkv.shape[0]
    kernel = functools.partial(_fox_kernel, tq=tq)
    return pl.pallas_call(
        kernel,
        grid=(batch, n_heads, nq),
        in_specs=[pl.BlockSpec((tq, HEAD_DIM), lambda b, h, i: (b * nq + i, q_col + h)),
                  pl.BlockSpec((seq, HEAD_DIM), lambda b, h, i: (b, k_col + h)),
                  pl.BlockSpec((seq, HEAD_DIM), lambda b, h, i: (b, v_col + h)),
                  pl.BlockSpec((1, 1, seq), lambda b, h, i: (b * n_heads + h, 0, 0))],
        out_specs=pl.BlockSpec((tq, HEAD_DIM), lambda b, h, i: (b * nq + i, h)),
        out_shape=jax.ShapeDtypeStruct((t, n_heads * HEAD_DIM), BF16),
        scratch_shapes=[pltpu.VMEM((tq, LANES), F32), pltpu.VMEM((tq, LANES), F32),
                        pltpu.VMEM((tq, HEAD_DIM), F32),
                        pltpu.VMEM((tq, tq), F32), pltpu.VMEM((tq, tq), F32)],
        compiler_params=_params(3, VMEM_LIMIT_BYTES),
        name="fox_attention",
    )(qkv, qkv, qkv, f_rows)


def _sb_kernel(q_ref, k_ref, v_ref, o_ref, c_ref, acc_ref, *, tq, tk):
    i = pl.program_id(2)
    q = q_ref[...]
    c_ref[...] = jnp.zeros_like(c_ref)
    acc_ref[...] = jnp.zeros_like(acc_ref)
    r = lax.broadcasted_iota(I32, (2 * tk, tk), 0)
    cc = lax.broadcasted_iota(I32, (2 * tk, tk), 1)
    upper2 = ((r & (tk - 1)) > cc).astype(BF16)
    nblk = (i + 1) * (tq // tk)
    col_minus_row = lax.broadcasted_iota(I32, (tq, tk), 1) - lax.broadcasted_iota(I32, (tq, tk), 0)

    def cond(carry):
        step, c_max = carry
        return jnp.logical_and(step < nblk, c_max > SB_EXIT_LOG2)

    def body(carry):
        step, _ = carry
        kj = nblk - 1 - step
        ks = pl.ds(pl.multiple_of(kj * tk, tk), tk)
        z = _dot_nt(q, k_ref[ks, :])
        strict = col_minus_row < (i * tq - kj * tk)
        ls = jnp.minimum(z, 0.0) - jnp.log2(1.0 + jnp.exp2(-jnp.abs(z)))
        lneg = jnp.where(strict, ls - z, 0.0)
        suffix = jnp.dot(jnp.concatenate(_split_bf16(lneg, 2), axis=1), upper2, preferred_element_type=F32)
        c_prev = c_ref[...]
        a = jnp.where(strict, jnp.exp2(ls + suffix + c_prev), 0.0)
        acc_ref[...] += jnp.dot(a.astype(BF16), v_ref[ks, :], preferred_element_type=F32)
        c_new = c_prev + suffix[:, 0:1] + lneg[:, 0:1]
        c_ref[...] = c_new
        return step + 1, jnp.max(c_new)

    lax.while_loop(cond, body, (jnp.int32(0), jnp.float32(0.0)))
    o_ref[...] = acc_ref[...].astype(o_ref.dtype)


def _sb_attention(qkv, batch, seq, n_heads, q_col, k_col, v_col):
    tq = min(512, seq)
    tk = min(256, seq)
    nq = seq // tq
    t = qkv.shape[0]
    kernel = functools.partial(_sb_kernel, tq=tq, tk=tk)
    return pl.pallas_call(
        kernel,
        grid=(batch, n_heads, nq),
        in_specs=[pl.BlockSpec((tq, HEAD_DIM), lambda b, h, i: (b * nq + i, q_col + h)),
                  pl.BlockSpec((seq, HEAD_DIM), lambda b, h, i: (b, k_col + h)),
                  pl.BlockSpec((seq, HEAD_DIM), lambda b, h, i: (b, v_col + h))],
        out_specs=pl.BlockSpec((tq, HEAD_DIM), lambda b, h, i: (b * nq + i, h)),
        out_shape=jax.ShapeDtypeStruct((t, n_heads * HEAD_DIM), BF16),
        scratch_shapes=[pltpu.VMEM((tq, 1), F32), pltpu.VMEM((tq, HEAD_DIM), F32)],
        compiler_params=_params(3, VMEM_LIMIT_BYTES),
        name="sb_attention",
    )(qkv, qkv, qkv)


def _dsa_kernel(q_ref, qi_ref, wi_ref, k_ref, v_ref, kia_ref, kib_ref, o_ref,
                keys_ref, keyst_ref, qs_ref, m_ref, acc_ref, sa_ref, sb_ref, *, n_heads, kc, kca, topk):
    i = pl.program_id(1)
    tq = Q_BLOCK
    nch = ((i + 1) * tq + kc - 1) // kc
    row_t = lax.broadcasted_iota(I32, (tq, kc), 0) + i * tq
    col_l = lax.broadcasted_iota(I32, (tq, kc), 1)

    def index_body(c, carry):
        ks = pl.ds(pl.multiple_of(c * kc, kc), kc)
        kia = kia_ref[ks, :]
        kib = kib_ref[ks, :]
        wi = wi_ref[...]
        score = jnp.zeros((tq, kc), F32)
        for p in range(IDX_HEADS // 2):
            qp = qi_ref[:, p * LANES:(p + 1) * LANES]
            score = score + jnp.maximum(_dot_nt(qp, kia), 0.0) * wi[:, 2 * p:2 * p + 1]
            score = score + jnp.maximum(_dot_nt(qp, kib), 0.0) * wi[:, 2 * p + 1:2 * p + 2]
        bits = lax.bitcast_convert_type(score + 0.0, I32)
        key = bits ^ ((bits >> 31) & 0x7FFFFFFF)
        adm = ((col_l + c * kc) >> CHUNK_SHIFT) <= (row_t >> CHUNK_SHIFT)
        key = jnp.where(adm, key, INT_MIN)
        keys_ref[:, ks] = key
        keyst_ref[ks, :] = key.T
        return carry

    lax.fori_loop(0, nch, index_body, 0)

    def count_ge(trial):
        def body(c, cnt):
            kk = keyst_ref[pl.ds(pl.multiple_of(c * kc, kc), kc), :]
            hit = jnp.where(kk >= trial, 1.0, 0.0)
            return cnt + jnp.sum(hit.reshape(kc // COUNT_ROWS, COUNT_ROWS, tq), axis=0)
        cnt = lax.fori_loop(0, nch, body, jnp.zeros((COUNT_ROWS, tq), F32))
        return jnp.sum(cnt, axis=0, keepdims=True)

    def per_row(v):
        return jnp.broadcast_to(v, (LANES, tq)).T

    kf = float(topk)
    cur_q = jnp.where(count_ge(jnp.zeros((1, tq), I32)) >= kf, 0, INT_MIN).astype(I32)

    def search_body(it, cur):
        trial = cur + jnp.left_shift(jnp.int32(1), 30 - it)
        return jnp.where(count_ge(trial) >= kf, trial, cur)

    cur_q = lax.fori_loop(0, 31, search_body, cur_q)
    cur = per_row(cur_q)
    thr = jnp.maximum(cur, INT_MIN + 1)

    surplus = jnp.where((count_ge(cur_q) > kf) & (cur_q > INT_MIN), 1.0, 0.0)

    @pl.when(jnp.max(surplus) > 0.0)
    def _():
        need = per_row(kf - count_ge(cur_q + 1))
        r = lax.broadcasted_iota(I32, (kc, kc), 0)
        cc = lax.broadcasted_iota(I32, (kc, kc), 1)
        before = (r < cc).astype(BF16)
        lane_tiles = [slice(t * LANES, (t + 1) * LANES) for t in range(kc // LANES)]

        def tie_body(c, run):
            base = pl.multiple_of(c * kc, kc)
            kk = keys_ref[:, pl.ds(base, kc)]
            eq = [kk[:, sl] == cur for sl in lane_tiles]
            eqf = [jnp.where(e, 1.0, 0.0) for e in eq]
            rank = jnp.dot(jnp.concatenate(eqf, axis=-1).astype(BF16), before, preferred_element_type=F32)
            for t, sl in enumerate(lane_tiles):
                retire = eq[t] & (rank[:, sl] + run >= need)
                keys_ref[:, pl.ds(base + t * LANES, LANES)] = jnp.where(retire, INT_MIN, kk[:, sl])
            tot = functools.reduce(jnp.add, eqf)
            return run + jnp.broadcast_to(jnp.sum(tot, axis=-1, keepdims=True), tot.shape)

        lax.fori_loop(0, nch, tie_body, jnp.zeros((tq, LANES), F32))

    for h in range(n_heads):
        qs_ref[h * tq:(h + 1) * tq, :] = q_ref[:, h * HEAD_DIM:(h + 1) * HEAD_DIM]
    m_ref[...] = jnp.full_like(m_ref, MASK_BIAS)
    acc_ref[...] = jnp.zeros_like(acc_ref)
    n_att = ((i + 1) * tq + kca - 1) // kca
    n_tiles = kca // LANES
    scored = nch * kc

    def chunk(c):
        return pl.ds(pl.multiple_of(c * kca, kca), kca)

    def scores(c, s_ref):
        s_ref[...] = _dot_nt(qs_ref[...], k_ref[chunk(jnp.minimum(c, n_att - 1)), :])

    def softmax_pv(c, s_ref):
        ks = chunk(c)
        lane = lax.broadcasted_iota(I32, (tq, LANES), 1)
        kk = keys_ref[:, ks]
        s = s_ref[...].reshape(n_heads, tq, kca)
        tiles = []
        for t in range(n_tiles):
            sl = slice(t * LANES, (t + 1) * LANES)
            bias = jnp.where((kk[:, sl] >= thr) & (lane < scored - c * kca - t * LANES), 0.0, MASK_BIAS)
            tiles.append(s[:, :, sl] + bias[None])
        m_new, alpha, p = _softmax_block(tiles, m_ref[...])
        pb = jnp.concatenate(p, axis=-1).reshape(n_heads * tq, kca).astype(BF16)
        pv = jnp.dot(pb, v_ref[ks, :], preferred_element_type=F32)
        a2 = alpha.reshape(n_heads * tq, LANES)
        acc_ref[...] = jnp.concatenate([a2, a2], axis=-1) * acc_ref[...] + pv
        m_ref[...] = m_new

    scores(0, sa_ref)

    def pair(p, carry):
        scores(2 * p + 1, sb_ref)
        softmax_pv(2 * p, sa_ref)
        scores(2 * p + 2, sa_ref)
        softmax_pv(2 * p + 1, sb_ref)
        return carry

    lax.fori_loop(0, n_att // 2, pair, 0)

    @pl.when(n_att % 2 == 1)
    def _():
        softmax_pv(n_att - 1, sa_ref)

    acc = acc_ref[...]
    out = acc[:, :HEAD_DIM] / acc[:, HEAD_DIM:]
    for h in range(n_heads):
        o_ref[:, h * HEAD_DIM:(h + 1) * HEAD_DIM] = out[h * tq:(h + 1) * tq, :].astype(o_ref.dtype)


def _dsa_attention(q, qi, wi, k, v, kia, kib, batch, seq, n_heads):
    tq = Q_BLOCK
    kc = min(256, seq)
    nq = seq // tq
    topk = min(TOPK_MAX, seq // 4)
    t = q.shape[0]
    kca = min(512, seq)
    kernel = functools.partial(_dsa_kernel, n_heads=n_heads, kc=kc, kca=kca, topk=topk)
    qblk = lambda w: pl.BlockSpec((tq, w), lambda b, i: (b * nq + i, 0))
    full = pl.BlockSpec((seq, LANES), lambda b, i: (b, 0))
    vext = pl.BlockSpec((seq, 2 * LANES), lambda b, i: (b, 0))
    return pl.pallas_call(
        kernel,
        grid=(batch, nq),
        in_specs=[qblk(n_heads * HEAD_DIM), qblk(IDX_HEADS * IDX_DIM), qblk(LANES), full, vext, full, full],
        out_specs=qblk(n_heads * HEAD_DIM),
        out_shape=jax.ShapeDtypeStruct((t, n_heads * HEAD_DIM), BF16),
        scratch_shapes=[pltpu.VMEM((tq, seq), I32),
                        pltpu.VMEM((seq, tq), I32),
                        pltpu.VMEM((n_heads * tq, HEAD_DIM), BF16),
                        pltpu.VMEM((n_heads, tq, LANES), F32),
                        pltpu.VMEM((n_heads * tq, 2 * LANES), F32),
                        pltpu.VMEM((n_heads * tq, kca), F32),
                        pltpu.VMEM((n_heads * tq, kca), F32)],
        compiler_params=_params(2, VMEM_LIMIT_BYTES),
        name="dsa_attention",
    )(q, qi, wi, k, v, kia, kib)


def _outproj_kernel(*refs, n_parts, gate_row):
    o_refs = refs[:n_parts]
    w_refs = refs[n_parts:2 * n_parts]
    x_ref, g_ref, mod_ref, out_ref = refs[2 * n_parts:]
    y = jnp.dot(o_refs[0][...], w_refs[0][...], preferred_element_type=F32)
    for o_r, w_r in zip(o_refs[1:], w_refs[1:]):
        y = y + jnp.dot(o_r[...], w_r[...], preferred_element_type=F32)
    out_ref[...] = x_ref[...] + mod_ref[0, gate_row:gate_row + 1, :] * _rms(y, g_ref[...])


def _outproj_residual(o_parts, w_parts, x2, g, mod, seq, gate_row):
    t, d = x2.shape
    tm = min(512, seq)
    nsb = seq // tm
    n_parts = len(o_parts)
    in_specs = [pl.BlockSpec((tm, o.shape[1]), lambda i: (i, 0)) for o in o_parts]
    in_specs += [pl.BlockSpec(w.shape, lambda i: (0, 0)) for w in w_parts]
    in_specs += [pl.BlockSpec((tm, d), lambda i: (i, 0)),
                 pl.BlockSpec((1, d), lambda i: (0, 0)),
                 pl.BlockSpec((1, 6, d), lambda i: (i // nsb, 0, 0))]
    return pl.pallas_call(
        functools.partial(_outproj_kernel, n_parts=n_parts, gate_row=gate_row),
        grid=(t // tm,),
        in_specs=in_specs,
        out_specs=pl.BlockSpec((tm, d), lambda i: (i, 0)),
        out_shape=jax.ShapeDtypeStruct((t, d), F32),
        compiler_params=_params(1, VMEM_LIMIT_BYTES),
        name="outproj_residual",
    )(*o_parts, *w_parts, x2, g.reshape(1, d), mod)


def _ffn_kernel(x_ref, g_in_ref, g_out_ref, mod_ref, w1_ref, w2_ref, out_ref, h_ref, acc_ref):
    j = pl.program_id(1)

    @pl.when(j == 0)
    def _():
        y = _rms(x_ref[...], g_in_ref[...])
        h_ref[...] = (y * (1.0 + mod_ref[0, 4:5, :]) + mod_ref[0, 3:4, :]).astype(h_ref.dtype)
        acc_ref[...] = jnp.zeros_like(acc_ref)

    u = jnp.maximum(jnp.dot(h_ref[...], w1_ref[...], preferred_element_type=F32), 0.0)
    acc_ref[...] += jnp.dot((u * u).astype(BF16), w2_ref[...], preferred_element_type=F32)

    @pl.when(j == pl.num_programs(1) - 1)
    def _():
        out_ref[...] = x_ref[...] + mod_ref[0, 5:6, :] * _rms(acc_ref[...], g_out_ref[...])


def _ffn_residual(x2, g_in, g_out, mod, w1, w2, seq):
    t, d = x2.shape
    f = w1.shape[1]
    tm = min(512, seq)
    tf = min(1024, f)
    nsb = seq // tm
    row = pl.BlockSpec((1, d), lambda i, j: (0, 0))
    return pl.pallas_call(
        _ffn_kernel,
        grid=(t // tm, f // tf),
        in_specs=[pl.BlockSpec((tm, d), lambda i, j: (i, 0)), row, row,
                  pl.BlockSpec((1, 6, d), lambda i, j: (i // nsb, 0, 0)),
                  pl.BlockSpec((d, tf), lambda i, j: (0, j)),
                  pl.BlockSpec((tf, d), lambda i, j: (j, 0))],
        out_specs=pl.BlockSpec((tm, d), lambda i, j: (i, 0)),
        out_shape=jax.ShapeDtypeStruct((t, d), F32),
        scratch_shapes=[pltpu.VMEM((tm, d), BF16), pltpu.VMEM((tm, d), F32)],
        compiler_params=_params(2, VMEM_LIMIT_BYTES),
        name="ffn_residual",
    )(x2, g_in.reshape(1, d), g_out.reshape(1, d), mod, w1, w2)


def _even_layer(x2, mod, norm_g, w_in, b_forget, w_out, batch, seq):
    d = x2.shape[1]
    n_heads = d // HEAD_DIM
    n_fox = n_heads // 2
    n_sb = n_heads - n_fox
    fw = n_fox * HEAD_DIM
    sw = n_sb * HEAD_DIM
    scale = HEAD_DIM ** -0.5
    w_main = jnp.concatenate([w_in[:, :fw] * (scale * LOG2E), w_in[:, fw:3 * fw],
                              w_in[:, 3 * fw + n_fox:3 * fw + n_fox + sw] * (scale * LOG2E),
                              w_in[:, 3 * fw + n_fox + sw:]], axis=1).astype(BF16)
    w_gate = jnp.pad(w_in[:, 3 * fw:3 * fw + n_fox], ((0, 0), (0, LANES - n_fox))).astype(BF16)
    b_row = jnp.pad(b_forget.astype(F32), (0, LANES - n_fox)).reshape(1, LANES)

    qkv, fg = _even_projection(x2, norm_g[0], mod, w_main, w_gate, seq)
    f_cum = _gate_cumsum(fg, b_row, batch, seq)
    f_rows = f_cum.reshape(batch, seq, LANES)[:, :, :n_fox].transpose(0, 2, 1).reshape(batch * n_fox, 1, seq)
    o_f = _fox_attention(qkv, f_rows, batch, seq, n_fox, 0, n_fox, 2 * n_fox)
    o_s = _sb_attention(qkv, batch, seq, n_sb, 3 * n_fox, 3 * n_fox + n_sb, 3 * n_fox + 2 * n_sb)
    w_o = w_out.astype(BF16)
    return _outproj_residual([o_f, o_s], [w_o[:fw], w_o[fw:]], x2, norm_g[1], mod, seq, gate_row=2)


def _odd_layer(x2, mod, norm_g, w_in, w_out, pos_col, batch, seq):
    d = x2.shape[1]
    n_heads = d // HEAD_DIM
    qw = n_heads * HEAD_DIM
    iw = IDX_HEADS * IDX_DIM
    o_k, o_v, o_qi, o_ki, o_wi = qw, qw + HEAD_DIM, qw + 2 * HEAD_DIM, qw + 2 * HEAD_DIM + iw, qw + 2 * HEAD_DIM + iw + IDX_DIM
    w_q = w_in[:, :qw].astype(BF16)
    w_qi = w_in[:, o_qi:o_ki].astype(BF16)
    w_kvi = jnp.concatenate([w_in[:, o_k:o_qi], w_in[:, o_ki:],
                             jnp.zeros((d, LANES - IDX_DIM - IDX_HEADS), w_in.dtype)], axis=1).astype(BF16)

    cos, sin = _rope_tables(pos_col, HEAD_DIM // 2)
    cosi, sini = _rope_tables(pos_col, IDX_DIM // 2)
    q, qi, k, v, kia, kib, wi = _odd_projection(
        x2, norm_g[0], mod, jnp.concatenate([w_q, w_qi], axis=1), w_kvi, cos, sin, cosi, sini, seq,
        q_width=qw, q_scale=HEAD_DIM ** -0.5 * LOG2E)
    o = _dsa_attention(q, qi, wi, k, v, kia, kib, batch, seq, n_heads)
    return _outproj_residual([o], [w_out.astype(BF16)], x2, norm_g[1], mod, seq, gate_row=2)


def kernel(x, c, positions, ada_w, ada_b, norm_g, mix_w_out, even_w_in, even_b_forget, odd_w_in, ff_w1, ff_w2):
    batch, seq, d = x.shape
    depth = ada_w.shape[0]
    assert d % HEAD_DIM == 0 and seq % Q_BLOCK == 0 and seq >= TOPK_MAX
    mods = _ada_mod(c, ada_w, ada_b).reshape(depth, batch, 6, d)
    pos_col = positions.reshape(batch * seq, 1)
    x2 = x.reshape(batch * seq, d)
    for l in range(depth):
        mod = mods[l]
        if l % 2 == 0:
            x2 = _even_layer(x2, mod, norm_g[l], even_w_in[l // 2], even_b_forget[l // 2],
                             mix_w_out[l], batch, seq)
        else:
            x2 = _odd_layer(x2, mod, norm_g[l], odd_w_in[l // 2], mix_w_out[l], pos_col, batch, seq)
        x2 = _ffn_residual(x2, norm_g[l, 2], norm_g[l, 3], mod,
                           ff_w1[l].astype(BF16), ff_w2[l].astype(BF16), seq)
    return x2.reshape(batch, seq, d)
```

```python
import functools

import jax
import jax.numpy as jnp
from jax import lax
from jax.experimental import pallas as pl
from jax.experimental.pallas import tpu as pltpu

F32 = jnp.float32
BF16 = jnp.bfloat16
I32 = jnp.int32

HEAD_DIM = 128
CHUNK = 64
CHUNK_SHIFT = 6
Q_BLOCK = 128
IDX_HEADS = 16
IDX_DIM = 64
TOPK_MAX = 256
ROPE_THETA = 10000.0
EPS = 1e-6

LANES = 128
SUBLANES = 8
COUNT_ROWS = 4 * SUBLANES
ROW_BLOCK = 2 * SUBLANES
ROW_UNROLL = 8
INT_MIN = -(2 ** 31)
MASK_BIAS = -1e30
SB_EXIT_LOG2 = -152.0
LOG2E = 1.4426950408889634
VMEM_LIMIT_BYTES = 56 * 1024 * 1024


def _params(n_axes, vmem=None):
    kw = dict(dimension_semantics=("arbitrary",) * n_axes)
    if vmem is not None:
        kw["vmem_limit_bytes"] = vmem
    return pltpu.CompilerParams(**kw)


def _dot_nt(a, b):
    return lax.dot_general(a, b, (((1,), (1,)), ((), ())), preferred_element_type=F32)


def _split_bf16(x, parts):
    out = []
    r = x
    for _ in range(parts):
        p = r.astype(BF16)
        out.append(p)
        r = r - p.astype(F32)
    return out


def _log_sigmoid(x):
    return jnp.minimum(x, 0.0) - jnp.log1p(jnp.exp(-jnp.abs(x)))


def _ada_kernel(c_ref, w_ref, b_ref, o_ref):
    c = c_ref[...]
    cs = c / (1.0 + jnp.exp(-c))
    o_ref[0] = jnp.dot(cs, w_ref[0], preferred_element_type=F32,
                       precision=lax.Precision.HIGHEST) + b_ref[0]


def _ada_mod(c, ada_w, ada_b):
    depth, d, n = ada_w.shape
    b = c.shape[0]
    tn = min(1024, n)
    return pl.pallas_call(
        _ada_kernel,
        grid=(depth, n // tn),
        in_specs=[pl.BlockSpec((b, d), lambda l, j: (0, 0)),
                  pl.BlockSpec((1, d, tn), lambda l, j: (l, 0, j)),
                  pl.BlockSpec((1, 1, tn), lambda l, j: (l, 0, j))],
        out_specs=pl.BlockSpec((1, b, tn), lambda l, j: (l, 0, j)),
        out_shape=jax.ShapeDtypeStruct((depth, b, n), F32),
        compiler_params=_params(2, VMEM_LIMIT_BYTES),
        name="ada_mod",
    )(c, ada_w, ada_b.reshape(depth, 1, n))


def _rms(x, g):
    ms = jnp.mean(x * x, axis=-1, keepdims=True)
    return x * lax.rsqrt(ms + EPS) * g


def _for_row_blocks(n_rows, fn):
    def body(r, carry):
        fn(pl.ds(pl.multiple_of(r * ROW_BLOCK, ROW_BLOCK), ROW_BLOCK))
        return carry
    lax.fori_loop(0, n_rows // ROW_BLOCK, body, 0, unroll=ROW_UNROLL)


def _modnorm_into(h_ref, x_ref, g_ref, mod_ref, sh_row, sc_row):
    gain = g_ref[...] * (1.0 + mod_ref[0, sc_row:sc_row + 1, :])
    shift = mod_ref[0, sh_row:sh_row + 1, :]

    def rows_fn(rows):
        h_ref[rows, :] = (_rms(x_ref[rows, :], gain) + shift).astype(h_ref.dtype)
    _for_row_blocks(x_ref.shape[0], rows_fn)


def _gated_residual_into(out_ref, x_ref, y_ref, g_ref, mod_ref, gate_row):
    gain = g_ref[...] * mod_ref[0, gate_row:gate_row + 1, :]

    def rows_fn(rows):
        out_ref[rows, :] = x_ref[rows, :] + _rms(y_ref[rows, :], gain)
    _for_row_blocks(x_ref.shape[0], rows_fn)


def _rope128(a, cos, sin_signed):
    return a * cos + pltpu.roll(a, HEAD_DIM // 2, 1) * sin_signed


def _rope64(a, cos, sin_signed):
    lane = lax.broadcasted_iota(I32, a.shape, 1)
    first_half = (lane & (IDX_DIM - 1)) < (IDX_DIM // 2)
    rot = jnp.where(first_half, pltpu.roll(a, LANES - IDX_DIM // 2, 1), pltpu.roll(a, IDX_DIM // 2, 1))
    return a * cos + rot * sin_signed


def _even_proj_kernel(x_ref, g_ref, mod_ref, w_ref, wg_ref, o_ref, fg_ref, h_ref):
    @pl.when(pl.program_id(1) == 0)
    def _():
        _modnorm_into(h_ref, x_ref, g_ref, mod_ref, 0, 1)
        fg_ref[...] = jnp.dot(h_ref[...], wg_ref[...], preferred_element_type=F32)

    o_ref[...] = jnp.dot(h_ref[...], w_ref[...], preferred_element_type=F32).astype(o_ref.dtype)


def _even_projection(x2, g, mod, w_main, w_gate, seq, tm=1024, tn=512):
    t, d = x2.shape
    n = w_main.shape[1]
    tm = min(tm, seq)
    tn = min(tn, n)
    nsb = seq // tm
    return pl.pallas_call(
        _even_proj_kernel,
        grid=(t // tm, n // tn),
        in_specs=[pl.BlockSpec((tm, d), lambda i, j: (i, 0)),
                  pl.BlockSpec((1, d), lambda i, j: (0, 0)),
                  pl.BlockSpec((1, 6, d), lambda i, j: (i // nsb, 0, 0)),
                  pl.BlockSpec((d, tn), lambda i, j: (0, j)),
                  pl.BlockSpec((d, LANES), lambda i, j: (0, 0))],
        out_specs=[pl.BlockSpec((tm, tn), lambda i, j: (i, j)),
                   pl.BlockSpec((tm, LANES), lambda i, j: (i, 0))],
        out_shape=[jax.ShapeDtypeStruct((t, n), BF16), jax.ShapeDtypeStruct((t, LANES), F32)],
        scratch_shapes=[pltpu.VMEM((tm, d), BF16)],
        compiler_params=_params(2, VMEM_LIMIT_BYTES),
        name="even_in_proj",
    )(x2, g.reshape(1, d), mod, w_main, w_gate)


def _odd_proj_kernel(x_ref, g_ref, mod_ref, w_ref, wkvi_ref, cos_ref, sin_ref, cosi_ref, sini_ref,
                     q_ref, qi_ref, k_ref, v_ref, kia_ref, kib_ref, wi_ref, h_ref, *, n_q_tiles, q_scale):
    j = pl.program_id(1)

    @pl.when(j == 0)
    def _():
        _modnorm_into(h_ref, x_ref, g_ref, mod_ref, 0, 1)
        acc = jnp.dot(h_ref[...], wkvi_ref[...], preferred_element_type=F32)
        k_ref[...] = _rope128(acc[:, :LANES], cos_ref[...], sin_ref[...]).astype(k_ref.dtype)
        v_ref[:, :LANES] = acc[:, LANES:2 * LANES].astype(v_ref.dtype)
        v_ref[:, LANES:] = jnp.ones((acc.shape[0], LANES), v_ref.dtype)
        t3 = acc[:, 2 * LANES:]
        lane = lax.broadcasted_iota(I32, t3.shape, 1)
        ki = jnp.where(lane < IDX_DIM, _rope64(t3, cosi_ref[...], sini_ref[...]), 0.0)
        kia_ref[...] = ki.astype(kia_ref.dtype)
        kib_ref[...] = pltpu.roll(ki, IDX_DIM, 1).astype(kib_ref.dtype)
        wi_ref[...] = pltpu.roll(t3, IDX_DIM, 1) * (IDX_HEADS ** -0.5 * IDX_DIM ** -0.5)

    acc = jnp.dot(h_ref[...], w_ref[...], preferred_element_type=F32)
    lane_tiles = [slice(t * LANES, (t + 1) * LANES) for t in range(acc.shape[1] // LANES)]

    @pl.when(j < n_q_tiles)
    def _():
        cos = cos_ref[...] * q_scale
        sin = sin_ref[...] * q_scale
        for sl in lane_tiles:
            q_ref[:, sl] = _rope128(acc[:, sl], cos, sin).astype(q_ref.dtype)

    @pl.when(j >= n_q_tiles)
    def _():
        for sl in lane_tiles:
            qi_ref[:, sl] = _rope64(acc[:, sl], cosi_ref[...], sini_ref[...]).astype(qi_ref.dtype)


def _odd_projection(x2, g, mod, w_qqi, w_kvi, cos, sin, cosi, sini, seq, q_width, q_scale, tm=1024, tn=512):
    t, d = x2.shape
    n = w_qqi.shape[1]
    tm = min(tm, seq)
    tn = min(tn, q_width)
    nsb = seq // tm
    n_q_tiles = q_width // tn
    tab = pl.BlockSpec((tm, LANES), lambda i, j: (i, 0))
    shp = lambda w, dt: jax.ShapeDtypeStruct((t, w), dt)
    return pl.pallas_call(
        functools.partial(_odd_proj_kernel, n_q_tiles=n_q_tiles, q_scale=q_scale),
        grid=(t // tm, n // tn),
        in_specs=[pl.BlockSpec((tm, d), lambda i, j: (i, 0)),
                  pl.BlockSpec((1, d), lambda i, j: (0, 0)),
                  pl.BlockSpec((1, 6, d), lambda i, j: (i // nsb, 0, 0)),
                  pl.BlockSpec((d, tn), lambda i, j: (0, j)),
                  pl.BlockSpec(w_kvi.shape, lambda i, j: (0, 0)), tab, tab, tab, tab],
        out_specs=[pl.BlockSpec((tm, tn), lambda i, j: (i, jnp.minimum(j, n_q_tiles - 1))),
                   pl.BlockSpec((tm, tn), lambda i, j: (i, jnp.maximum(j - n_q_tiles, 0))),
                   tab, pl.BlockSpec((tm, 2 * LANES), lambda i, j: (i, 0)), tab, tab, tab],
        out_shape=[shp(q_width, BF16), shp(n - q_width, BF16), shp(LANES, BF16), shp(2 * LANES, BF16),
                   shp(LANES, BF16), shp(LANES, BF16), shp(LANES, F32)],
        scratch_shapes=[pltpu.VMEM((tm, d), BF16)],
        compiler_params=_params(2, VMEM_LIMIT_BYTES),
        name="odd_in_proj",
    )(x2, g.reshape(1, d), mod, w_qqi, w_kvi, cos, sin, cosi, sini)


def _rope_tab_kernel(pos_ref, inv_ref, sgn_ref, cos_ref, sin_ref):
    ang = pos_ref[...].astype(F32) * inv_ref[...]
    cos_ref[...] = jnp.cos(ang)
    sin_ref[...] = jnp.sin(ang) * sgn_ref[...]


def _rope_tables(pos_col, half):
    t = pos_col.shape[0]
    inv = ROPE_THETA ** (-jnp.arange(half, dtype=F32) / half)
    reps = LANES // (2 * half)
    inv_row = jnp.tile(jnp.concatenate([inv, inv]), reps).reshape(1, LANES)
    sgn_row = jnp.tile(jnp.concatenate([-jnp.ones(half, F32), jnp.ones(half, F32)]), reps).reshape(1, LANES)
    tm = min(1024, t)
    row = pl.BlockSpec((1, LANES), lambda i: (0, 0))
    tab = pl.BlockSpec((tm, LANES), lambda i: (i, 0))
    return pl.pallas_call(
        _rope_tab_kernel,
        grid=(t // tm,),
        in_specs=[pl.BlockSpec((tm, 1), lambda i: (i, 0)), row, row],
        out_specs=[tab, tab],
        out_shape=[jax.ShapeDtypeStruct((t, LANES), F32)] * 2,
        compiler_params=_params(1),
        name="rope_tables",
    )(pos_col, inv_row, sgn_row)


def _gate_cumsum_kernel(fg_ref, b_ref, f_ref, carry_ref):
    @pl.when(pl.program_id(1) == 0)
    def _():
        carry_ref[...] = jnp.zeros_like(carry_ref)

    lf = _log_sigmoid(fg_ref[...] + b_ref[...])
    tc = lf.shape[0]
    r = lax.broadcasted_iota(I32, (tc, tc), 0)
    c = lax.broadcasted_iota(I32, (tc, tc), 1)
    tri = (c <= r).astype(BF16)
    cs = carry_ref[...]
    for piece in _split_bf16(lf, 3):
        cs = cs + jnp.dot(tri, piece, preferred_element_type=F32)
    f_ref[...] = cs * LOG2E
    carry_ref[...] = cs[tc - 1:tc, :]


def _gate_cumsum(fg, b_row, batch, seq):
    tc = min(256, seq)
    nsb = seq // tc
    return pl.pallas_call(
        _gate_cumsum_kernel,
        grid=(batch, nsb),
        in_specs=[pl.BlockSpec((tc, LANES), lambda b, j: (b * nsb + j, 0)),
                  pl.BlockSpec((1, LANES), lambda b, j: (0, 0))],
        out_specs=pl.BlockSpec((tc, LANES), lambda b, j: (b * nsb + j, 0)),
        out_shape=jax.ShapeDtypeStruct(fg.shape, F32),
        scratch_shapes=[pltpu.VMEM((1, LANES), F32)],
        compiler_params=_params(2),
        name="gate_cumsum",
    )(fg, b_row)


def _softmax_block(tiles, m_prev):
    mx = functools.reduce(jnp.maximum, tiles)
    m_new = jnp.maximum(m_prev, jnp.broadcast_to(jnp.max(mx, axis=-1, keepdims=True), mx.shape))
    alpha = jnp.exp2(m_prev - m_new)
    return m_new, alpha, [jnp.exp2(t - m_new) for t in tiles]


def _fox_kernel(q_ref, k_ref, v_ref, f_ref, o_ref, m_ref, l_ref, acc_ref, sa_ref, sb_ref, *, tq):
    i = pl.program_id(2)
    m_ref[...] = jnp.full_like(m_ref, -jnp.inf)
    l_ref[...] = jnp.zeros_like(l_ref)
    acc_ref[...] = jnp.zeros_like(acc_ref)
    n_tiles = tq // LANES

    def keys(kj):
        return pl.ds(pl.multiple_of(kj * tq, tq), tq)

    def scores(kj, s_ref):
        s_ref[...] = _dot_nt(q_ref[...], k_ref[keys(kj), :])

    def softmax_pv(kj, s_ref, masked):
        ks = keys(kj)
        s = s_ref[...] - f_ref[0, :, ks]
        if masked:
            row = lax.broadcasted_iota(I32, s.shape, 0)
            col = lax.broadcasted_iota(I32, s.shape, 1)
            s = jnp.where(col <= row, s, -jnp.inf)
        tiles = [s[:, t * LANES:(t + 1) * LANES] for t in range(n_tiles)]
        m_new, alpha, p = _softmax_block(tiles, m_ref[...])
        psum = functools.reduce(jnp.add, p)
        l_ref[...] = alpha * l_ref[...] + jnp.broadcast_to(jnp.sum(psum, axis=-1, keepdims=True), psum.shape)
        pv = jnp.dot(jnp.concatenate(p, axis=-1).astype(BF16), v_ref[ks, :], preferred_element_type=F32)
        acc_ref[...] = alpha * acc_ref[...] + pv
        m_ref[...] = m_new

    scores(0, sa_ref)

    def pair(p, carry):
        scores(2 * p + 1, sb_ref)
        softmax_pv(2 * p, sa_ref, masked=False)
        scores(2 * p + 2, sa_ref)
        softmax_pv(2 * p + 1, sb_ref, masked=False)
        return carry

    lax.fori_loop(0, i // 2, pair, 0)

    @pl.when(i % 2 == 0)
    def _():
        softmax_pv(i, sa_ref, masked=True)

    @pl.when(i % 2 == 1)
    def _():
        scores(i, sb_ref)
        softmax_pv(i - 1, sa_ref, masked=False)
        softmax_pv(i, sb_ref, masked=True)

    o_ref[...] = (acc_ref[...] / l_ref[...]).astype(o_ref.dtype)


def _fox_attention(qkv, f_rows, batch, seq, n_heads, q_col, k_col, v_col):
    tq = min(512, seq)
    nq = seq // tq
    t = qkv.shape[0]
    kernel = functools.partial(_fox_kernel, tq=tq)
    return pl.pallas_call(
        kernel,
        grid=(batch, n_heads, nq),
        in_specs=[pl.BlockSpec((tq, HEAD_DIM), lambda b, h, i: (b * nq + i, q_col + h)),
                  pl.BlockSpec((seq, HEAD_DIM), lambda b, h, i: (b, k_col + h)),
                  pl.BlockSpec((seq, HEAD_DIM), lambda b, h, i: (b, v_col + h)),
                  pl.BlockSpec((1, 1, seq), lambda b, h, i: (b * n_heads + h, 0, 0))],
        out_specs=pl.BlockSpec((tq, HEAD_DIM), lambda b, h, i: (b * nq + i, h)),
        out_shape=jax.ShapeDtypeStruct((t, n_heads * HEAD_DIM), BF16),
        scratch_shapes=[pltpu.VMEM((tq, LANES), F32), pltpu.VMEM((tq, LANES), F32),
                        pltpu.VMEM((tq, HEAD_DIM), F32),
                        pltpu.VMEM((tq, tq), F32), pltpu.VMEM((tq, tq), F32)],
        compiler_params=_params(3, VMEM_LIMIT_BYTES),
        name="fox_attention",
    )(qkv, qkv, qkv, f_rows)


def _sb_kernel(q_ref, k_ref, v_ref, o_ref, c_ref, acc_ref, *, tq, tk, hg):
    i = pl.program_id(2)
    c_ref[...] = jnp.zeros_like(c_ref)
    acc_ref[...] = jnp.zeros_like(acc_ref)
    r = lax.broadcasted_iota(I32, (2 * tk, tk), 0)
    cc = lax.broadcasted_iota(I32, (2 * tk, tk), 1)
    upper2 = ((r & (tk - 1)) > cc).astype(BF16)
    nblk = (i + 1) * (tq // tk)
    col_minus_row = lax.broadcasted_iota(I32, (tq, tk), 1) - lax.broadcasted_iota(I32, (tq, tk), 0)

    def cond(carry):
        step, c_max = carry
        return jnp.logical_and(step < nblk, c_max > SB_EXIT_LOG2)

    def body(carry):
        step, _ = carry
        kj = nblk - 1 - step
        ks = pl.ds(pl.multiple_of(kj * tk, tk), tk)
        strict = col_minus_row < (i * tq - kj * tk)
        c_max = None
        for g in range(hg):
            hd = slice(g * HEAD_DIM, (g + 1) * HEAD_DIM)
            z = _dot_nt(q_ref[:, hd], k_ref[ks, hd])
            ls = jnp.minimum(z, 0.0) - jnp.log2(1.0 + jnp.exp2(-jnp.abs(z)))
            lneg = jnp.where(strict, ls - z, 0.0)
            suffix = jnp.dot(jnp.concatenate(_split_bf16(lneg, 2), axis=1), upper2, preferred_element_type=F32)
            c_prev = c_ref[g]
            a = jnp.where(strict, jnp.exp2(ls + suffix + c_prev), 0.0)
            acc_ref[:, hd] += jnp.dot(a.astype(BF16), v_ref[ks, hd], preferred_element_type=F32)
            c_new = c_prev + suffix[:, 0:1] + lneg[:, 0:1]
            c_ref[g] = c_new
            c_max = jnp.max(c_new) if c_max is None else jnp.maximum(c_max, jnp.max(c_new))
        return step + 1, c_max

    lax.while_loop(cond, body, (jnp.int32(0), jnp.float32(0.0)))
    o_ref[...] = acc_ref[...].astype(o_ref.dtype)


def _sb_attention(qkv, batch, seq, n_heads, q_col, k_col, v_col):
    tq = min(512, seq)
    tk = min(256, seq)
    nq = seq // tq
    t = qkv.shape[0]
    hg = 2 if (n_heads % 2 == 0 and q_col % 2 == 0 and k_col % 2 == 0 and v_col % 2 == 0) else 1
    w = hg * HEAD_DIM
    kernel = functools.partial(_sb_kernel, tq=tq, tk=tk, hg=hg)
    return pl.pallas_call(
        kernel,
        grid=(batch, n_heads // hg, nq),
        in_specs=[pl.BlockSpec((tq, w), lambda b, h, i: (b * nq + i, q_col // hg + h)),
                  pl.BlockSpec((seq, w), lambda b, h, i: (b, k_col // hg + h)),
                  pl.BlockSpec((seq, w), lambda b, h, i: (b, v_col // hg + h))],
        out_specs=pl.BlockSpec((tq, w), lambda b, h, i: (b * nq + i, h)),
        out_shape=jax.ShapeDtypeStruct((t, n_heads * HEAD_DIM), BF16),
        scratch_shapes=[pltpu.VMEM((hg, tq, 1), F32), pltpu.VMEM((tq, w), F32)],
        compiler_params=_params(3, VMEM_LIMIT_BYTES),
        name="sb_attention",
    )(qkv, qkv, qkv)


def _dsa_kernel(q_ref, qi_ref, wi_ref, k_ref, v_ref, kia_ref, kib_ref, o_ref,
                keys_ref, keyst_ref, qs_ref, m_ref, acc_ref, sa_ref, sb_ref, *, n_heads, kc, kca, topk):
    i = pl.program_id(1)
    tq = Q_BLOCK
    nch = ((i + 1) * tq + kc - 1) // kc
    row_t = lax.broadcasted_iota(I32, (tq, kc), 0) + i * tq
    col_l = lax.broadcasted_iota(I32, (tq, kc), 1)

    def index_body(c, carry):
        ks = pl.ds(pl.multiple_of(c * kc, kc), kc)
        kia = kia_ref[ks, :]
        kib = kib_ref[ks, :]
        wi = wi_ref[...]
        score = jnp.zeros((tq, kc), F32)
        for p in range(IDX_HEADS // 2):
            qp = qi_ref[:, p * LANES:(p + 1) * LANES]
            score = score + jnp.maximum(_dot_nt(qp, kia), 0.0) * wi[:, 2 * p:2 * p + 1]
            score = score + jnp.maximum(_dot_nt(qp, kib), 0.0) * wi[:, 2 * p + 1:2 * p + 2]
        bits = lax.bitcast_convert_type(score + 0.0, I32)
        key = bits ^ ((bits >> 31) & 0x7FFFFFFF)
        adm = ((col_l + c * kc) >> CHUNK_SHIFT) <= (row_t >> CHUNK_SHIFT)
        key = jnp.where(adm, key, INT_MIN)
        keys_ref[:, ks] = key
        keyst_ref[ks, :] = key.T
        return carry

    lax.fori_loop(0, nch, index_body, 0)

    def count_ge(trial):
        def body(c, cnt):
            kk = keyst_ref[pl.ds(pl.multiple_of(c * kc, kc), kc), :]
            hit = jnp.where(kk >= trial, 1.0, 0.0)
            return cnt + jnp.sum(hit.reshape(kc // COUNT_ROWS, COUNT_ROWS, tq), axis=0)
        cnt = lax.fori_loop(0, nch, body, jnp.zeros((COUNT_ROWS, tq), F32))
        return jnp.sum(cnt, axis=0, keepdims=True)

    def per_row(v):
        return jnp.broadcast_to(v, (LANES, tq)).T

    kf = float(topk)
    cur_q = jnp.where(count_ge(jnp.zeros((1, tq), I32)) >= kf, 0, INT_MIN).astype(I32)

    def search_body(it, cur):
        trial = cur + jnp.left_shift(jnp.int32(1), 30 - it)
        return jnp.where(count_ge(trial) >= kf, trial, cur)

    cur_q = lax.fori_loop(0, 31, search_body, cur_q)
    cur = per_row(cur_q)
    thr = jnp.maximum(cur, INT_MIN + 1)

    surplus = jnp.where((count_ge(cur_q) > kf) & (cur_q > INT_MIN), 1.0, 0.0)

    @pl.when(jnp.max(surplus) > 0.0)
    def _():
        need = per_row(kf - count_ge(cur_q + 1))
        r = lax.broadcasted_iota(I32, (kc, kc), 0)
        cc = lax.broadcasted_iota(I32, (kc, kc), 1)
        before = (r < cc).astype(BF16)
        lane_tiles = [slice(t * LANES, (t + 1) * LANES) for t in range(kc // LANES)]

        def tie_body(c, run):
            base = pl.multiple_of(c * kc, kc)
            kk = keys_ref[:, pl.ds(base, kc)]
            eq = [kk[:, sl] == cur for sl in lane_tiles]
            eqf = [jnp.where(e, 1.0, 0.0) for e in eq]
            rank = jnp.dot(jnp.concatenate(eqf, axis=-1).astype(BF16), before, preferred_element_type=F32)
            for t, sl in enumerate(lane_tiles):
                retire = eq[t] & (rank[:, sl] + run >= need)
                keys_ref[:, pl.ds(base + t * LANES, LANES)] = jnp.where(retire, INT_MIN, kk[:, sl])
            tot = functools.reduce(jnp.add, eqf)
            return run + jnp.broadcast_to(jnp.sum(tot, axis=-1, keepdims=True), tot.shape)

        lax.fori_loop(0, nch, tie_body, jnp.zeros((tq, LANES), F32))

    for h in range(n_heads):
        qs_ref[h * tq:(h + 1) * tq, :] = q_ref[:, h * HEAD_DIM:(h + 1) * HEAD_DIM]
    m_ref[...] = jnp.full_like(m_ref, MASK_BIAS)
    acc_ref[...] = jnp.zeros_like(acc_ref)
    n_att = ((i + 1) * tq + kca - 1) // kca
    n_tiles = kca // LANES
    scored = nch * kc

    def chunk(c):
        return pl.ds(pl.multiple_of(c * kca, kca), kca)

    def scores(c, s_ref):
        s_ref[...] = _dot_nt(qs_ref[...], k_ref[chunk(jnp.minimum(c, n_att - 1)), :])

    def softmax_pv(c, s_ref):
        ks = chunk(c)
        lane = lax.broadcasted_iota(I32, (tq, LANES), 1)
        kk = keys_ref[:, ks]
        s = s_ref[...].reshape(n_heads, tq, kca)
        tiles = []
        for t in range(n_tiles):
            sl = slice(t * LANES, (t + 1) * LANES)
            bias = jnp.where((kk[:, sl] >= thr) & (lane < scored - c * kca - t * LANES), 0.0, MASK_BIAS)
            tiles.append(s[:, :, sl] + bias[None])
        m_new, alpha, p = _softmax_block(tiles, m_ref[...])
        pb = jnp.concatenate(p, axis=-1).reshape(n_heads * tq, kca).astype(BF16)
        pv = jnp.dot(pb, v_ref[ks, :], preferred_element_type=F32)
        a2 = alpha.reshape(n_heads * tq, LANES)
        acc_ref[...] = jnp.concatenate([a2, a2], axis=-1) * acc_ref[...] + pv
        m_ref[...] = m_new

    scores(0, sa_ref)

    def pair(p, carry):
        scores(2 * p + 1, sb_ref)
        softmax_pv(2 * p, sa_ref)
        scores(2 * p + 2, sa_ref)
        softmax_pv(2 * p + 1, sb_ref)
        return carry

    lax.fori_loop(0, n_att // 2, pair, 0)

    @pl.when(n_att % 2 == 1)
    def _():
        softmax_pv(n_att - 1, sa_ref)

    acc = acc_ref[...]
    out = acc[:, :HEAD_DIM] / acc[:, HEAD_DIM:]
    for h in range(n_heads):
        o_ref[:, h * HEAD_DIM:(h + 1) * HEAD_DIM] = out[h * tq:(h + 1) * tq, :].astype(o_ref.dtype)


def _dsa_attention(q, qi, wi, k, v, kia, kib, batch, seq, n_heads):
    tq = Q_BLOCK
    kc = min(256, seq)
    nq = seq // tq
    topk = min(TOPK_MAX, seq // 4)
    t = q.shape[0]
    kca = min(512, seq)
    kernel = functools.partial(_dsa_kernel, n_heads=n_heads, kc=kc, kca=kca, topk=topk)
    qblk = lambda w: pl.BlockSpec((tq, w), lambda b, i: (b * nq + i, 0))
    full = pl.BlockSpec((seq, LANES), lambda b, i: (b, 0))
    vext = pl.BlockSpec((seq, 2 * LANES), lambda b, i: (b, 0))
    return pl.pallas_call(
        kernel,
        grid=(batch, nq),
        in_specs=[qblk(n_heads * HEAD_DIM), qblk(IDX_HEADS * IDX_DIM), qblk(LANES), full, vext, full, full],
        out_specs=qblk(n_heads * HEAD_DIM),
        out_shape=jax.ShapeDtypeStruct((t, n_heads * HEAD_DIM), BF16),
        scratch_shapes=[pltpu.VMEM((tq, seq), I32),
                        pltpu.VMEM((seq, tq), I32),
                        pltpu.VMEM((n_heads * tq, HEAD_DIM), BF16),
                        pltpu.VMEM((n_heads, tq, LANES), F32),
                        pltpu.VMEM((n_heads * tq, 2 * LANES), F32),
                        pltpu.VMEM((n_heads * tq, kca), F32),
                        pltpu.VMEM((n_heads * tq, kca), F32)],
        compiler_params=_params(2, VMEM_LIMIT_BYTES),
        name="dsa_attention",
    )(q, qi, wi, k, v, kia, kib)


def _outproj_kernel(*refs, n_parts, gate_row):
    o_refs = refs[:n_parts]
    w_refs = refs[n_parts:2 * n_parts]
    x_ref, g_ref, mod_ref, out_ref, y_ref = refs[2 * n_parts:]
    y = jnp.dot(o_refs[0][...], w_refs[0][...], preferred_element_type=F32)
    for o_r, w_r in zip(o_refs[1:], w_refs[1:]):
        y = y + jnp.dot(o_r[...], w_r[...], preferred_element_type=F32)
    y_ref[...] = y
    _gated_residual_into(out_ref, x_ref, y_ref, g_ref, mod_ref, gate_row)


def _outproj_residual(o_parts, w_parts, x2, g, mod, seq, gate_row):
    t, d = x2.shape
    tm = min(512, seq)
    nsb = seq // tm
    n_parts = len(o_parts)
    in_specs = [pl.BlockSpec((tm, o.shape[1]), lambda i: (i, 0)) for o in o_parts]
    in_specs += [pl.BlockSpec(w.shape, lambda i: (0, 0)) for w in w_parts]
    in_specs += [pl.BlockSpec((tm, d), lambda i: (i, 0)),
                 pl.BlockSpec((1, d), lambda i: (0, 0)),
                 pl.BlockSpec((1, 6, d), lambda i: (i // nsb, 0, 0))]
    return pl.pallas_call(
        functools.partial(_outproj_kernel, n_parts=n_parts, gate_row=gate_row),
        grid=(t // tm,),
        in_specs=in_specs,
        out_specs=pl.BlockSpec((tm, d), lambda i: (i, 0)),
        out_shape=jax.ShapeDtypeStruct((t, d), F32),
        scratch_shapes=[pltpu.VMEM((tm, d), F32)],
        compiler_params=_params(1, VMEM_LIMIT_BYTES),
        name="outproj_residual",
    )(*o_parts, *w_parts, x2, g.reshape(1, d), mod)


def _ffn_kernel(x_ref, g_in_ref, g_out_ref, mod_ref, w1_ref, w2_ref, out_ref, h_ref, acc_ref):
    j = pl.program_id(1)

    @pl.when(j == 0)
    def _():
        _modnorm_into(h_ref, x_ref, g_in_ref, mod_ref, 3, 4)
        acc_ref[...] = jnp.zeros_like(acc_ref)

    u = jnp.maximum(jnp.dot(h_ref[...], w1_ref[...], preferred_element_type=F32), 0.0)
    acc_ref[...] += jnp.dot((u * u).astype(BF16), w2_ref[...], preferred_element_type=F32)

    @pl.when(j == pl.num_programs(1) - 1)
    def _():
        _gated_residual_into(out_ref, x_ref, acc_ref, g_out_ref, mod_ref, 5)


def _ffn_residual(x2, g_in, g_out, mod, w1, w2, seq):
    t, d = x2.shape
    f = w1.shape[1]
    tm = min(512, seq)
    tf = min(1024, f)
    nsb = seq // tm
    row = pl.BlockSpec((1, d), lambda i, j: (0, 0))
    return pl.pallas_call(
        _ffn_kernel,
        grid=(t // tm, f // tf),
        in_specs=[pl.BlockSpec((tm, d), lambda i, j: (i, 0)), row, row,
                  pl.BlockSpec((1, 6, d), lambda i, j: (i // nsb, 0, 0)),
                  pl.BlockSpec((d, tf), lambda i, j: (0, j)),
                  pl.BlockSpec((tf, d), lambda i, j: (j, 0))],
        out_specs=pl.BlockSpec((tm, d), lambda i, j: (i, 0)),
        out_shape=jax.ShapeDtypeStruct((t, d), F32),
        scratch_shapes=[pltpu.VMEM((tm, d), BF16), pltpu.VMEM((tm, d), F32)],
        compiler_params=_params(2, VMEM_LIMIT_BYTES),
        name="ffn_residual",
    )(x2, g_in.reshape(1, d), g_out.reshape(1, d), mod, w1, w2)


def _even_layer(x2, mod, norm_g, w_in, b_forget, w_out, batch, seq):
    d = x2.shape[1]
    n_heads = d // HEAD_DIM
    n_fox = n_heads // 2
    n_sb = n_heads - n_fox
    fw = n_fox * HEAD_DIM
    sw = n_sb * HEAD_DIM
    scale = HEAD_DIM ** -0.5
    w_main = jnp.concatenate([w_in[:, :fw] * (scale * LOG2E), w_in[:, fw:3 * fw],
                              w_in[:, 3 * fw + n_fox:3 * fw + n_fox + sw] * (scale * LOG2E),
                              w_in[:, 3 * fw + n_fox + sw:]], axis=1).astype(BF16)
    w_gate = jnp.pad(w_in[:, 3 * fw:3 * fw + n_fox], ((0, 0), (0, LANES - n_fox))).astype(BF16)
    b_row = jnp.pad(b_forget.astype(F32), (0, LANES - n_fox)).reshape(1, LANES)

    qkv, fg = _even_projection(x2, norm_g[0], mod, w_main, w_gate, seq)
    f_cum = _gate_cumsum(fg, b_row, batch, seq)
    f_rows = f_cum.reshape(batch, seq, LANES)[:, :, :n_fox].transpose(0, 2, 1).reshape(batch * n_fox, 1, seq)
    o_f = _fox_attention(qkv, f_rows, batch, seq, n_fox, 0, n_fox, 2 * n_fox)
    o_s = _sb_attention(qkv, batch, seq, n_sb, 3 * n_fox, 3 * n_fox + n_sb, 3 * n_fox + 2 * n_sb)
    w_o = w_out.astype(BF16)
    return _outproj_residual([o_f, o_s], [w_o[:fw], w_o[fw:]], x2, norm_g[1], mod, seq, gate_row=2)


def _odd_layer(x2, mod, norm_g, w_in, w_out, pos_col, batch, seq):
    d = x2.shape[1]
    n_heads = d // HEAD_DIM
    qw = n_heads * HEAD_DIM
    iw = IDX_HEADS * IDX_DIM
    o_k, o_v, o_qi, o_ki, o_wi = qw, qw + HEAD_DIM, qw + 2 * HEAD_DIM, qw + 2 * HEAD_DIM + iw, qw + 2 * HEAD_DIM + iw + IDX_DIM
    w_q = w_in[:, :qw].astype(BF16)
    w_qi = w_in[:, o_qi:o_ki].astype(BF16)
    w_kvi = jnp.concatenate([w_in[:, o_k:o_qi], w_in[:, o_ki:],
                             jnp.zeros((d, LANES - IDX_DIM - IDX_HEADS), w_in.dtype)], axis=1).astype(BF16)

    cos, sin = _rope_tables(pos_col, HEAD_DIM // 2)
    cosi, sini = _rope_tables(pos_col, IDX_DIM // 2)
    q, qi, k, v, kia, kib, wi = _odd_projection(
        x2, norm_g[0], mod, jnp.concatenate([w_q, w_qi], axis=1), w_kvi, cos, sin, cosi, sini, seq,
        q_width=qw, q_scale=HEAD_DIM ** -0.5 * LOG2E)
    o = _dsa_attention(q, qi, wi, k, v, kia, kib, batch, seq, n_heads)
    return _outproj_residual([o], [w_out.astype(BF16)], x2, norm_g[1], mod, seq, gate_row=2)


def kernel(x, c, positions, ada_w, ada_b, norm_g, mix_w_out, even_w_in, even_b_forget, odd_w_in, ff_w1, ff_w2):
    batch, seq, d = x.shape
    depth = ada_w.shape[0]
    assert d % HEAD_DIM == 0 and seq % Q_BLOCK == 0 and seq >= TOPK_MAX
    mods = _ada_mod(c, ada_w, ada_b).reshape(depth, batch, 6, d)
    pos_col = positions.reshape(batch * seq, 1)
    x2 = x.reshape(batch * seq, d)
    for l in range(depth):
        mod = mods[l]
        if l % 2 == 0:
            x2 = _even_layer(x2, mod, norm_g[l], even_w_in[l // 2], even_b_forget[l // 2],
                             mix_w_out[l], batch, seq)
        else:
            x2 = _odd_layer(x2, mod, norm_g[l], odd_w_in[l // 2], mix_w_out[l], pos_col, batch, seq)
        x2 = _ffn_residual(x2, norm_g[l, 2], norm_g[l, 3], mod,
                           ff_w1[l].astype(BF16), ff_w2[l].astype(BF16), seq)
    return x2.reshape(batch, seq, d)
```

```python
import functools

import jax
import jax.numpy as jnp
from jax import lax
from jax.experimental import pallas as pl
from jax.experimental.pallas import tpu as pltpu

F32 = jnp.float32
BF16 = jnp.bfloat16
I32 = jnp.int32

HEAD_DIM = 128
CHUNK = 64
CHUNK_SHIFT = 6
Q_BLOCK = 128
IDX_HEADS = 16
IDX_DIM = 64
TOPK_MAX = 256
ROPE_THETA = 10000.0
EPS = 1e-6

LANES = 128
SUBLANES = 8
COUNT_ROWS = 4 * SUBLANES
ROW_BLOCK = 2 * SUBLANES
ROW_UNROLL = 8
INT_MIN = -(2 ** 31)
MASK_BIAS = -1e30
SB_EXIT_LOG2 = -152.0
LOG2E = 1.4426950408889634
VMEM_LIMIT_BYTES = 56 * 1024 * 1024


def _params(n_axes, vmem=None):
    kw = dict(dimension_semantics=("arbitrary",) * n_axes)
    if vmem is not None:
        kw["vmem_limit_bytes"] = vmem
    return pltpu.CompilerParams(**kw)


def _dot_nt(a, b):
    return lax.dot_general(a, b, (((1,), (1,)), ((), ())), preferred_element_type=F32)


def _split_bf16(x, parts):
    out = []
    r = x
    for _ in range(parts):
        p = r.astype(BF16)
        out.append(p)
        r = r - p.astype(F32)
    return out


def _log_sigmoid(x):
    return jnp.minimum(x, 0.0) - jnp.log1p(jnp.exp(-jnp.abs(x)))


def _ada_kernel(c_ref, w_ref, b_ref, o_ref):
    c = c_ref[...]
    cs = c / (1.0 + jnp.exp(-c))
    o_ref[0] = jnp.dot(cs, w_ref[0], preferred_element_type=F32,
                       precision=lax.Precision.HIGHEST) + b_ref[0]


def _ada_mod(c, ada_w, ada_b):
    depth, d, n = ada_w.shape
    b = c.shape[0]
    tn = min(1024, n)
    return pl.pallas_call(
        _ada_kernel,
        grid=(depth, n // tn),
        in_specs=[pl.BlockSpec((b, d), lambda l, j: (0, 0)),
                  pl.BlockSpec((1, d, tn), lambda l, j: (l, 0, j)),
                  pl.BlockSpec((1, 1, tn), lambda l, j: (l, 0, j))],
        out_specs=pl.BlockSpec((1, b, tn), lambda l, j: (l, 0, j)),
        out_shape=jax.ShapeDtypeStruct((depth, b, n), F32),
        compiler_params=_params(2, VMEM_LIMIT_BYTES),
        name="ada_mod",
    )(c, ada_w, ada_b.reshape(depth, 1, n))


def _rms(x, g):
    ms = jnp.mean(x * x, axis=-1, keepdims=True)
    return x * lax.rsqrt(ms + EPS) * g


def _for_row_blocks(n_rows, fn):
    def body(r, carry):
        fn(pl.ds(pl.multiple_of(r * ROW_BLOCK, ROW_BLOCK), ROW_BLOCK))
        return carry
    lax.fori_loop(0, n_rows // ROW_BLOCK, body, 0, unroll=ROW_UNROLL)


def _modulated_norm(x, g, mod_ref, sh_row, sc_row):
    return _rms(x, g) * (1.0 + mod_ref[0, sc_row:sc_row + 1, :]) + mod_ref[0, sh_row:sh_row + 1, :]


def _modnorm_into(h_ref, x_ref, g_ref, mod_ref, sh_row, sc_row):
    gain = g_ref[...] * (1.0 + mod_ref[0, sc_row:sc_row + 1, :])
    shift = mod_ref[0, sh_row:sh_row + 1, :]

    def rows_fn(rows):
        h_ref[rows, :] = (_rms(x_ref[rows, :], gain) + shift).astype(h_ref.dtype)
    _for_row_blocks(x_ref.shape[0], rows_fn)


def _gated_residual_into(out_ref, x_ref, y_ref, g_ref, mod_ref, gate_row):
    gain = g_ref[...] * mod_ref[0, gate_row:gate_row + 1, :]

    def rows_fn(rows):
        out_ref[rows, :] = x_ref[rows, :] + _rms(y_ref[rows, :], gain)
    _for_row_blocks(x_ref.shape[0], rows_fn)


def _rope128(a, cos, sin_signed):
    return a * cos + pltpu.roll(a, HEAD_DIM // 2, 1) * sin_signed


def _rope64(a, cos, sin_signed):
    lane = lax.broadcasted_iota(I32, a.shape, 1)
    first_half = (lane & (IDX_DIM - 1)) < (IDX_DIM // 2)
    rot = jnp.where(first_half, pltpu.roll(a, LANES - IDX_DIM // 2, 1), pltpu.roll(a, IDX_DIM // 2, 1))
    return a * cos + rot * sin_signed


def _even_proj_kernel(x_ref, g_ref, mod_ref, w_ref, wg_ref, o_ref, fg_ref, h_ref):
    @pl.when(pl.program_id(1) == 0)
    def _():
        h = _modulated_norm(x_ref[...], g_ref[...], mod_ref, 0, 1).astype(h_ref.dtype)
        h_ref[...] = h
        fg_ref[...] = jnp.dot(h, wg_ref[...], preferred_element_type=F32)

    o_ref[...] = jnp.dot(h_ref[...], w_ref[...], preferred_element_type=F32).astype(o_ref.dtype)


def _even_projection(x2, g, mod, w_main, w_gate, seq, tm=1024, tn=512):
    t, d = x2.shape
    n = w_main.shape[1]
    tm = min(tm, seq)
    tn = min(tn, n)
    nsb = seq // tm
    return pl.pallas_call(
        _even_proj_kernel,
        grid=(t // tm, n // tn),
        in_specs=[pl.BlockSpec((tm, d), lambda i, j: (i, 0)),
                  pl.BlockSpec((1, d), lambda i, j: (0, 0)),
                  pl.BlockSpec((1, 6, d), lambda i, j: (i // nsb, 0, 0)),
                  pl.BlockSpec((d, tn), lambda i, j: (0, j)),
                  pl.BlockSpec((d, LANES), lambda i, j: (0, 0))],
        out_specs=[pl.BlockSpec((tm, tn), lambda i, j: (i, j)),
                   pl.BlockSpec((tm, LANES), lambda i, j: (i, 0))],
        out_shape=[jax.ShapeDtypeStruct((t, n), BF16), jax.ShapeDtypeStruct((t, LANES), F32)],
        scratch_shapes=[pltpu.VMEM((tm, d), BF16)],
        compiler_params=_params(2, VMEM_LIMIT_BYTES),
        name="even_in_proj",
    )(x2, g.reshape(1, d), mod, w_main, w_gate)


def _odd_proj_kernel(x_ref, g_ref, mod_ref, w_ref, wkvi_ref, cos_ref, sin_ref, cosi_ref, sini_ref,
                     q_ref, qi_ref, k_ref, v_ref, kia_ref, kib_ref, wi_ref, h_ref, *, n_q_tiles, q_scale):
    j = pl.program_id(1)

    @pl.when(j == 0)
    def _():
        h = _modulated_norm(x_ref[...], g_ref[...], mod_ref, 0, 1).astype(h_ref.dtype)
        h_ref[...] = h
        acc = jnp.dot(h, wkvi_ref[...], preferred_element_type=F32)
        k_ref[...] = _rope128(acc[:, :LANES], cos_ref[...], sin_ref[...]).astype(k_ref.dtype)
        v_ref[:, :LANES] = acc[:, LANES:2 * LANES].astype(v_ref.dtype)
        v_ref[:, LANES:] = jnp.ones((acc.shape[0], LANES), v_ref.dtype)
        t3 = acc[:, 2 * LANES:]
        lane = lax.broadcasted_iota(I32, t3.shape, 1)
        ki = jnp.where(lane < IDX_DIM, _rope64(t3, cosi_ref[...], sini_ref[...]), 0.0)
        kia_ref[...] = ki.astype(kia_ref.dtype)
        kib_ref[...] = pltpu.roll(ki, IDX_DIM, 1).astype(kib_ref.dtype)
        wi_ref[...] = pltpu.roll(t3, IDX_DIM, 1) * (IDX_HEADS ** -0.5 * IDX_DIM ** -0.5)

    acc = jnp.dot(h_ref[...], w_ref[...], preferred_element_type=F32)
    lane_tiles = [slice(t * LANES, (t + 1) * LANES) for t in range(acc.shape[1] // LANES)]

    @pl.when(j < n_q_tiles)
    def _():
        cos = cos_ref[...] * q_scale
        sin = sin_ref[...] * q_scale
        for sl in lane_tiles:
            q_ref[:, sl] = _rope128(acc[:, sl], cos, sin).astype(q_ref.dtype)

    @pl.when(j >= n_q_tiles)
    def _():
        for sl in lane_tiles:
            qi_ref[:, sl] = _rope64(acc[:, sl], cosi_ref[...], sini_ref[...]).astype(qi_ref.dtype)


def _odd_projection(x2, g, mod, w_qqi, w_kvi, cos, sin, cosi, sini, seq, q_width, q_scale, tm=1024, tn=512):
    t, d = x2.shape
    n = w_qqi.shape[1]
    tm = min(tm, seq)
    tn = min(tn, q_width)
    nsb = seq // tm
    n_q_tiles = q_width // tn
    tab = pl.BlockSpec((tm, LANES), lambda i, j: (i, 0))
    shp = lambda w, dt: jax.ShapeDtypeStruct((t, w), dt)
    return pl.pallas_call(
        functools.partial(_odd_proj_kernel, n_q_tiles=n_q_tiles, q_scale=q_scale),
        grid=(t // tm, n // tn),
        in_specs=[pl.BlockSpec((tm, d), lambda i, j: (i, 0)),
                  pl.BlockSpec((1, d), lambda i, j: (0, 0)),
                  pl.BlockSpec((1, 6, d), lambda i, j: (i // nsb, 0, 0)),
                  pl.BlockSpec((d, tn), lambda i, j: (0, j)),
                  pl.BlockSpec(w_kvi.shape, lambda i, j: (0, 0)), tab, tab, tab, tab],
        out_specs=[pl.BlockSpec((tm, tn), lambda i, j: (i, jnp.minimum(j, n_q_tiles - 1))),
                   pl.BlockSpec((tm, tn), lambda i, j: (i, jnp.maximum(j - n_q_tiles, 0))),
                   tab, pl.BlockSpec((tm, 2 * LANES), lambda i, j: (i, 0)), tab, tab, tab],
        out_shape=[shp(q_width, BF16), shp(n - q_width, BF16), shp(LANES, BF16), shp(2 * LANES, BF16),
                   shp(LANES, BF16), shp(LANES, BF16), shp(LANES, F32)],
        scratch_shapes=[pltpu.VMEM((tm, d), BF16)],
        compiler_params=_params(2, VMEM_LIMIT_BYTES),
        name="odd_in_proj",
    )(x2, g.reshape(1, d), mod, w_qqi, w_kvi, cos, sin, cosi, sini)


def _rope_tab_kernel(pos_ref, inv_ref, sgn_ref, cos_ref, sin_ref):
    ang = pos_ref[...].astype(F32) * inv_ref[...]
    cos_ref[...] = jnp.cos(ang)
    sin_ref[...] = jnp.sin(ang) * sgn_ref[...]


def _rope_tables(pos_col, half):
    t = pos_col.shape[0]
    inv = ROPE_THETA ** (-jnp.arange(half, dtype=F32) / half)
    reps = LANES // (2 * half)
    inv_row = jnp.tile(jnp.concatenate([inv, inv]), reps).reshape(1, LANES)
    sgn_row = jnp.tile(jnp.concatenate([-jnp.ones(half, F32), jnp.ones(half, F32)]), reps).reshape(1, LANES)
    tm = min(1024, t)
    row = pl.BlockSpec((1, LANES), lambda i: (0, 0))
    tab = pl.BlockSpec((tm, LANES), lambda i: (i, 0))
    return pl.pallas_call(
        _rope_tab_kernel,
        grid=(t // tm,),
        in_specs=[pl.BlockSpec((tm, 1), lambda i: (i, 0)), row, row],
        out_specs=[tab, tab],
        out_shape=[jax.ShapeDtypeStruct((t, LANES), F32)] * 2,
        compiler_params=_params(1),
        name="rope_tables",
    )(pos_col, inv_row, sgn_row)


def _gate_cumsum_kernel(fg_ref, b_ref, f_ref, carry_ref):
    @pl.when(pl.program_id(1) == 0)
    def _():
        carry_ref[...] = jnp.zeros_like(carry_ref)

    lf = _log_sigmoid(fg_ref[...] + b_ref[...])
    tc = lf.shape[0]
    r = lax.broadcasted_iota(I32, (tc, tc), 0)
    c = lax.broadcasted_iota(I32, (tc, tc), 1)
    tri = (c <= r).astype(BF16)
    cs = carry_ref[...]
    for piece in _split_bf16(lf, 3):
        cs = cs + jnp.dot(tri, piece, preferred_element_type=F32)
    f_ref[...] = cs * LOG2E
    carry_ref[...] = cs[tc - 1:tc, :]


def _gate_cumsum(fg, b_row, batch, seq):
    tc = min(256, seq)
    nsb = seq // tc
    return pl.pallas_call(
        _gate_cumsum_kernel,
        grid=(batch, nsb),
        in_specs=[pl.BlockSpec((tc, LANES), lambda b, j: (b * nsb + j, 0)),
                  pl.BlockSpec((1, LANES), lambda b, j: (0, 0))],
        out_specs=pl.BlockSpec((tc, LANES), lambda b, j: (b * nsb + j, 0)),
        out_shape=jax.ShapeDtypeStruct(fg.shape, F32),
        scratch_shapes=[pltpu.VMEM((1, LANES), F32)],
        compiler_params=_params(2),
        name="gate_cumsum",
    )(fg, b_row)


def _softmax_block(tiles, m_prev):
    mx = functools.reduce(jnp.maximum, tiles)
    m_new = jnp.maximum(m_prev, jnp.broadcast_to(jnp.max(mx, axis=-1, keepdims=True), mx.shape))
    alpha = jnp.exp2(m_prev - m_new)
    return m_new, alpha, [jnp.exp2(t - m_new) for t in tiles]


def _fox_kernel(q_ref, k_ref, v_ref, f_ref, o_ref, m_ref, l_ref, acc_ref, sa_ref, sb_ref, *, tq):
    i = pl.program_id(2)
    m_ref[...] = jnp.full_like(m_ref, -jnp.inf)
    l_ref[...] = jnp.zeros_like(l_ref)
    acc_ref[...] = jnp.zeros_like(acc_ref)
    n_tiles = tq // LANES

    def keys(kj):
        return pl.ds(pl.multiple_of(kj * tq, tq), tq)

    def scores(kj, s_ref):
        s_ref[...] = _dot_nt(q_ref[...], k_ref[keys(kj), :])

    def softmax_pv(kj, s_ref, masked):
        ks = keys(kj)
        s = s_ref[...] - f_ref[0, :, ks]
        if masked:
            row = lax.broadcasted_iota(I32, s.shape, 0)
            col = lax.broadcasted_iota(I32, s.shape, 1)
            s = jnp.where(col <= row, s, -jnp.inf)
        tiles = [s[:, t * LANES:(t + 1) * LANES] for t in range(n_tiles)]
        m_new, alpha, p = _softmax_block(tiles, m_ref[...])
        psum = functools.reduce(jnp.add, p)
        l_ref[...] = alpha * l_ref[...] + jnp.broadcast_to(jnp.sum(psum, axis=-1, keepdims=True), psum.shape)
        pv = jnp.dot(jnp.concatenate(p, axis=-1).astype(BF16), v_ref[ks, :], preferred_element_type=F32)
        acc_ref[...] = alpha * acc_ref[...] + pv
        m_ref[...] = m_new

    scores(0, sa_ref)

    def pair(p, carry):
        scores(2 * p + 1, sb_ref)
        softmax_pv(2 * p, sa_ref, masked=False)
        scores(2 * p + 2, sa_ref)
        softmax_pv(2 * p + 1, sb_ref, masked=False)
        return carry

    lax.fori_loop(0, i // 2, pair, 0)

    @pl.when(i % 2 == 0)
    def _():
        softmax_pv(i, sa_ref, masked=True)

    @pl.when(i % 2 == 1)
    def _():
        scores(i, sb_ref)
        softmax_pv(i - 1, sa_ref, masked=False)
        softmax_pv(i, sb_ref, masked=True)

    o_ref[...] = (acc_ref[...] / l_ref[...]).astype(o_ref.dtype)


def _fox_attention(qkv, f_rows, batch, seq, n_heads, q_col, k_col, v_col):
    tq = min(512, seq)
    nq = seq // tq
    t = qkv.shape[0]
    kernel = functools.partial(_fox_kernel, tq=tq)
    return pl.pallas_call(
        kernel,
        grid=(batch, n_heads, nq),
        in_specs=[pl.BlockSpec((tq, HEAD_DIM), lambda b, h, i: (b * nq + i, q_col + h)),
                  pl.BlockSpec((seq, HEAD_DIM), lambda b, h, i: (b, k_col + h)),
                  pl.BlockSpec((seq, HEAD_DIM), lambda b, h, i: (b, v_col + h)),
                  pl.BlockSpec((1, 1, seq), lambda b, h, i: (b * n_heads + h, 0, 0))],
        out_specs=pl.BlockSpec((tq, HEAD_DIM), lambda b, h, i: (b * nq + i, h)),
        out_shape=jax.ShapeDtypeStruct((t, n_heads * HEAD_DIM), BF16),
        scratch_shapes=[pltpu.VMEM((tq, LANES), F32), pltpu.VMEM((tq, LANES), F32),
                        pltpu.VMEM((tq, HEAD_DIM), F32),
                        pltpu.VMEM((tq, tq), F32), pltpu.VMEM((tq, tq), F32)],
        compiler_params=_params(3, VMEM_LIMIT_BYTES),
        name="fox_attention",
    )(qkv, qkv, qkv, f_rows)


def _sb_kernel(q_ref, k_ref, v_ref, o_ref, c_ref, acc_ref, *, tq, tk, hg):
    i = pl.program_id(2)
    c_ref[...] = jnp.zeros_like(c_ref)
    acc_ref[...] = jnp.zeros_like(acc_ref)
    r = lax.broadcasted_iota(I32, (2 * tk, tk), 0)
    cc = lax.broadcasted_iota(I32, (2 * tk, tk), 1)
    upper2 = ((r & (tk - 1)) > cc).astype(BF16)
    nblk = (i + 1) * (tq // tk)
    col_minus_row = lax.broadcasted_iota(I32, (tq, tk), 1) - lax.broadcasted_iota(I32, (tq, tk), 0)

    def cond(carry):
        step, c_max = carry
        return jnp.logical_and(step < nblk, c_max > SB_EXIT_LOG2)

    def body(carry):
        step, _ = carry
        kj = nblk - 1 - step
        ks = pl.ds(pl.multiple_of(kj * tk, tk), tk)
        strict = col_minus_row < (i * tq - kj * tk)
        c_max = None
        for g in range(hg):
            hd = slice(g * HEAD_DIM, (g + 1) * HEAD_DIM)
            z = _dot_nt(q_ref[:, hd], k_ref[ks, hd])
            ls = jnp.minimum(z, 0.0) - jnp.log2(1.0 + jnp.exp2(-jnp.abs(z)))
            lneg = jnp.where(strict, ls - z, 0.0)
            suffix = jnp.dot(jnp.concatenate(_split_bf16(lneg, 2), axis=1), upper2, preferred_element_type=F32)
            c_prev = c_ref[g]
            a = jnp.where(strict, jnp.exp2(ls + suffix + c_prev), 0.0)
            acc_ref[:, hd] += jnp.dot(a.astype(BF16), v_ref[ks, hd], preferred_element_type=F32)
            c_new = c_prev + suffix[:, 0:1] + lneg[:, 0:1]
            c_ref[g] = c_new
            c_max = jnp.max(c_new) if c_max is None else jnp.maximum(c_max, jnp.max(c_new))
        return step + 1, c_max

    lax.while_loop(cond, body, (jnp.int32(0), jnp.float32(0.0)))
    o_ref[...] = acc_ref[...].astype(o_ref.dtype)


def _sb_attention(qkv, batch, seq, n_heads, q_col, k_col, v_col):
    tq = min(512, seq)
    tk = min(256, seq)
    nq = seq // tq
    t = qkv.shape[0]
    hg = 2 if (n_heads % 2 == 0 and q_col % 2 == 0 and k_col % 2 == 0 and v_col % 2 == 0) else 1
    w = hg * HEAD_DIM
    kernel = functools.partial(_sb_kernel, tq=tq, tk=tk, hg=hg)
    return pl.pallas_call(
        kernel,
        grid=(batch, n_heads // hg, nq),
        in_specs=[pl.BlockSpec((tq, w), lambda b, h, i: (b * nq + i, q_col // hg + h)),
                  pl.BlockSpec((seq, w), lambda b, h, i: (b, k_col // hg + h)),
                  pl.BlockSpec((seq, w), lambda b, h, i: (b, v_col // hg + h))],
        out_specs=pl.BlockSpec((tq, w), lambda b, h, i: (b * nq + i, h)),
        out_shape=jax.ShapeDtypeStruct((t, n_heads * HEAD_DIM), BF16),
        scratch_shapes=[pltpu.VMEM((hg, tq, 1), F32), pltpu.VMEM((tq, w), F32)],
        compiler_params=_params(3, VMEM_LIMIT_BYTES),
        name="sb_attention",
    )(qkv, qkv, qkv)


def _dsa_kernel(q_ref, qi_ref, wi_ref, k_ref, v_ref, kia_ref, kib_ref, o_ref,
                keys_ref, keyst_ref, qs_ref, m_ref, acc_ref, sa_ref, sb_ref, *, n_heads, kc, kca, topk):
    i = pl.program_id(1)
    tq = Q_BLOCK
    nch = ((i + 1) * tq + kc - 1) // kc
    row_t = lax.broadcasted_iota(I32, (tq, kc), 0) + i * tq
    col_l = lax.broadcasted_iota(I32, (tq, kc), 1)

    def index_body(c, carry):
        ks = pl.ds(pl.multiple_of(c * kc, kc), kc)
        kia = kia_ref[ks, :]
        kib = kib_ref[ks, :]
        wi = wi_ref[...]
        score = jnp.zeros((tq, kc), F32)
        for p in range(IDX_HEADS // 2):
            qp = qi_ref[:, p * LANES:(p + 1) * LANES]
            score = score + jnp.maximum(_dot_nt(qp, kia), 0.0) * wi[:, 2 * p:2 * p + 1]
            score = score + jnp.maximum(_dot_nt(qp, kib), 0.0) * wi[:, 2 * p + 1:2 * p + 2]
        bits = lax.bitcast_convert_type(score + 0.0, I32)
        key = bits ^ ((bits >> 31) & 0x7FFFFFFF)
        adm = ((col_l + c * kc) >> CHUNK_SHIFT) <= (row_t >> CHUNK_SHIFT)
        key = jnp.where(adm, key, INT_MIN)
        keys_ref[:, ks] = key
        keyst_ref[ks, :] = key.T
        return carry

    lax.fori_loop(0, nch, index_body, 0)

    def count_ge(trial):
        def body(c, cnt):
            kk = keyst_ref[pl.ds(pl.multiple_of(c * kc, kc), kc), :]
            hit = jnp.where(kk >= trial, 1.0, 0.0)
            return cnt + jnp.sum(hit.reshape(kc // COUNT_ROWS, COUNT_ROWS, tq), axis=0)
        cnt = lax.fori_loop(0, nch, body, jnp.zeros((COUNT_ROWS, tq), F32))
        return jnp.sum(cnt, axis=0, keepdims=True)

    def per_row(v):
        return jnp.broadcast_to(v, (LANES, tq)).T

    kf = float(topk)
    cur_q = jnp.where(count_ge(jnp.zeros((1, tq), I32)) >= kf, 0, INT_MIN).astype(I32)

    def search_body(it, cur):
        trial = cur + jnp.left_shift(jnp.int32(1), 30 - it)
        return jnp.where(count_ge(trial) >= kf, trial, cur)

    cur_q = lax.fori_loop(0, 31, search_body, cur_q)
    cur = per_row(cur_q)
    thr = jnp.maximum(cur, INT_MIN + 1)

    surplus = jnp.where((count_ge(cur_q) > kf) & (cur_q > INT_MIN), 1.0, 0.0)

    @pl.when(jnp.max(surplus) > 0.0)
    def _():
        need = per_row(kf - count_ge(cur_q + 1))
        r = lax.broadcasted_iota(I32, (kc, kc), 0)
        cc = lax.broadcasted_iota(I32, (kc, kc), 1)
        before = (r < cc).astype(BF16)
        lane_tiles = [slice(t * LANES, (t + 1) * LANES) for t in range(kc // LANES)]

        def tie_body(c, run):
            base = pl.multiple_of(c * kc, kc)
            kk = keys_ref[:, pl.ds(base, kc)]
            eq = [kk[:, sl] == cur for sl in lane_tiles]
            eqf = [jnp.where(e, 1.0, 0.0) for e in eq]
            rank = jnp.dot(jnp.concatenate(eqf, axis=-1).astype(BF16), before, preferred_element_type=F32)
            for t, sl in enumerate(lane_tiles):
                retire = eq[t] & (rank[:, sl] + run >= need)
                keys_ref[:, pl.ds(base + t * LANES, LANES)] = jnp.where(retire, INT_MIN, kk[:, sl])
            tot = functools.reduce(jnp.add, eqf)
            return run + jnp.broadcast_to(jnp.sum(tot, axis=-1, keepdims=True), tot.shape)

        lax.fori_loop(0, nch, tie_body, jnp.zeros((tq, LANES), F32))

    for h in range(n_heads):
        qs_ref[h * tq:(h + 1) * tq, :] = q_ref[:, h * HEAD_DIM:(h + 1) * HEAD_DIM]
    m_ref[...] = jnp.full_like(m_ref, MASK_BIAS)
    acc_ref[...] = jnp.zeros_like(acc_ref)
    n_att = ((i + 1) * tq + kca - 1) // kca
    n_tiles = kca // LANES
    scored = nch * kc

    def chunk(c):
        return pl.ds(pl.multiple_of(c * kca, kca), kca)

    def scores(c, s_ref):
        s_ref[...] = _dot_nt(qs_ref[...], k_ref[chunk(jnp.minimum(c, n_att - 1)), :])

    def softmax_pv(c, s_ref):
        ks = chunk(c)
        lane = lax.broadcasted_iota(I32, (tq, LANES), 1)
        kk = keys_ref[:, ks]
        s = s_ref[...].reshape(n_heads, tq, kca)
        tiles = []
        for t in range(n_tiles):
            sl = slice(t * LANES, (t + 1) * LANES)
            bias = jnp.where((kk[:, sl] >= thr) & (lane < scored - c * kca - t * LANES), 0.0, MASK_BIAS)
            tiles.append(s[:, :, sl] + bias[None])
        m_new, alpha, p = _softmax_block(tiles, m_ref[...])
        pb = jnp.concatenate(p, axis=-1).reshape(n_heads * tq, kca).astype(BF16)
        pv = jnp.dot(pb, v_ref[ks, :], preferred_element_type=F32)
        a2 = alpha.reshape(n_heads * tq, LANES)
        acc_ref[...] = jnp.concatenate([a2, a2], axis=-1) * acc_ref[...] + pv
        m_ref[...] = m_new

    scores(0, sa_ref)

    def pair(p, carry):
        scores(2 * p + 1, sb_ref)
        softmax_pv(2 * p, sa_ref)
        scores(2 * p + 2, sa_ref)
        softmax_pv(2 * p + 1, sb_ref)
        return carry

    lax.fori_loop(0, n_att // 2, pair, 0)

    @pl.when(n_att % 2 == 1)
    def _():
        softmax_pv(n_att - 1, sa_ref)

    acc = acc_ref[...]
    out = acc[:, :HEAD_DIM] / acc[:, HEAD_DIM:]
    for h in range(n_heads):
        o_ref[:, h * HEAD_DIM:(h + 1) * HEAD_DIM] = out[h * tq:(h + 1) * tq, :].astype(o_ref.dtype)


def _dsa_attention(q, qi, wi, k, v, kia, kib, batch, seq, n_heads):
    tq = Q_BLOCK
    kc = min(256, seq)
    nq = seq // tq
    topk = min(TOPK_MAX, seq // 4)
    t = q.shape[0]
    kca = min(512, seq)
    kernel = functools.partial(_dsa_kernel, n_heads=n_heads, kc=kc, kca=kca, topk=topk)
    qblk = lambda w: pl.BlockSpec((tq, w), lambda b, i: (b * nq + i, 0))
    full = pl.BlockSpec((seq, LANES), lambda b, i: (b, 0))
    vext = pl.BlockSpec((seq, 2 * LANES), lambda b, i: (b, 0))
    return pl.pallas_call(
        kernel,
        grid=(batch, nq),
        in_specs=[qblk(n_heads * HEAD_DIM), qblk(IDX_HEADS * IDX_DIM), qblk(LANES), full, vext, full, full],
        out_specs=qblk(n_heads * HEAD_DIM),
        out_shape=jax.ShapeDtypeStruct((t, n_heads * HEAD_DIM), BF16),
        scratch_shapes=[pltpu.VMEM((tq, seq), I32),
                        pltpu.VMEM((seq, tq), I32),
                        pltpu.VMEM((n_heads * tq, HEAD_DIM), BF16),
                        pltpu.VMEM((n_heads, tq, LANES), F32),
                        pltpu.VMEM((n_heads * tq, 2 * LANES), F32),
                        pltpu.VMEM((n_heads * tq, kca), F32),
                        pltpu.VMEM((n_heads * tq, kca), F32)],
        compiler_params=_params(2, VMEM_LIMIT_BYTES),
        name="dsa_attention",
    )(q, qi, wi, k, v, kia, kib)


def _outproj_kernel(*refs, n_parts, gate_row):
    o_refs = refs[:n_parts]
    w_refs = refs[n_parts:2 * n_parts]
    x_ref, g_ref, mod_ref, out_ref = refs[2 * n_parts:]
    y = jnp.dot(o_refs[0][...], w_refs[0][...], preferred_element_type=F32)
    for o_r, w_r in zip(o_refs[1:], w_refs[1:]):
        y = y + jnp.dot(o_r[...], w_r[...], preferred_element_type=F32)
    out_ref[...] = x_ref[...] + mod_ref[0, gate_row:gate_row + 1, :] * _rms(y, g_ref[...])


def _outproj_residual(o_parts, w_parts, x2, g, mod, seq, gate_row):
    t, d = x2.shape
    tm = min(512, seq)
    nsb = seq // tm
    n_parts = len(o_parts)
    in_specs = [pl.BlockSpec((tm, o.shape[1]), lambda i: (i, 0)) for o in o_parts]
    in_specs += [pl.BlockSpec(w.shape, lambda i: (0, 0)) for w in w_parts]
    in_specs += [pl.BlockSpec((tm, d), lambda i: (i, 0)),
                 pl.BlockSpec((1, d), lambda i: (0, 0)),
                 pl.BlockSpec((1, 6, d), lambda i: (i // nsb, 0, 0))]
    return pl.pallas_call(
        functools.partial(_outproj_kernel, n_parts=n_parts, gate_row=gate_row),
        grid=(t // tm,),
        in_specs=in_specs,
        out_specs=pl.BlockSpec((tm, d), lambda i: (i, 0)),
        out_shape=jax.ShapeDtypeStruct((t, d), F32),
        compiler_params=_params(1, VMEM_LIMIT_BYTES),
        name="outproj_residual",
    )(*o_parts, *w_parts, x2, g.reshape(1, d), mod)


def _ffn_kernel(x_ref, g_in_ref, g_out_ref, mod_ref, w1_ref, w2_ref, out_ref, h_ref, acc_ref):
    j = pl.program_id(1)

    @pl.when(j == 0)
    def _():
        _modnorm_into(h_ref, x_ref, g_in_ref, mod_ref, 3, 4)
        acc_ref[...] = jnp.zeros_like(acc_ref)

    u = jnp.maximum(jnp.dot(h_ref[...], w1_ref[...], preferred_element_type=F32), 0.0)
    acc_ref[...] += jnp.dot((u * u).astype(BF16), w2_ref[...], preferred_element_type=F32)

    @pl.when(j == pl.num_programs(1) - 1)
    def _():
        _gated_residual_into(out_ref, x_ref, acc_ref, g_out_ref, mod_ref, 5)


def _ffn_residual(x2, g_in, g_out, mod, w1, w2, seq):
    t, d = x2.shape
    f = w1.shape[1]
    tm = min(512, seq)
    tf = min(1024, f)
    nsb = seq // tm
    row = pl.BlockSpec((1, d), lambda i, j: (0, 0))
    return pl.pallas_call(
        _ffn_kernel,
        grid=(t // tm, f // tf),
        in_specs=[pl.BlockSpec((tm, d), lambda i, j: (i, 0)), row, row,
                  pl.BlockSpec((1, 6, d), lambda i, j: (i // nsb, 0, 0)),
                  pl.BlockSpec((d, tf), lambda i, j: (0, j)),
                  pl.BlockSpec((tf, d), lambda i, j: (j, 0))],
        out_specs=pl.BlockSpec((tm, d), lambda i, j: (i, 0)),
        out_shape=jax.ShapeDtypeStruct((t, d), F32),
        scratch_shapes=[pltpu.VMEM((tm, d), BF16), pltpu.VMEM((tm, d), F32)],
        compiler_params=_params(2, VMEM_LIMIT_BYTES),
        name="ffn_residual",
    )(x2, g_in.reshape(1, d), g_out.reshape(1, d), mod, w1, w2)


def _even_layer(x2, mod, norm_g, w_in, b_forget, w_out, batch, seq):
    d = x2.shape[1]
    n_heads = d // HEAD_DIM
    n_fox = n_heads // 2
    n_sb = n_heads - n_fox
    fw = n_fox * HEAD_DIM
    sw = n_sb * HEAD_DIM
    scale = HEAD_DIM ** -0.5
    w_main = jnp.concatenate([w_in[:, :fw] * (scale * LOG2E), w_in[:, fw:3 * fw],
                              w_in[:, 3 * fw + n_fox:3 * fw + n_fox + sw] * (scale * LOG2E),
                              w_in[:, 3 * fw + n_fox + sw:]], axis=1).astype(BF16)
    w_gate = jnp.pad(w_in[:, 3 * fw:3 * fw + n_fox], ((0, 0), (0, LANES - n_fox))).astype(BF16)
    b_row = jnp.pad(b_forget.astype(F32), (0, LANES - n_fox)).reshape(1, LANES)

    qkv, fg = _even_projection(x2, norm_g[0], mod, w_main, w_gate, seq)
    f_cum = _gate_cumsum(fg, b_row, batch, seq)
    f_rows = f_cum.reshape(batch, seq, LANES)[:, :, :n_fox].transpose(0, 2, 1).reshape(batch * n_fox, 1, seq)
    o_f = _fox_attention(qkv, f_rows, batch, seq, n_fox, 0, n_fox, 2 * n_fox)
    o_s = _sb_attention(qkv, batch, seq, n_sb, 3 * n_fox, 3 * n_fox + n_sb, 3 * n_fox + 2 * n_sb)
    w_o = w_out.astype(BF16)
    return _outproj_residual([o_f, o_s], [w_o[:fw], w_o[fw:]], x2, norm_g[1], mod, seq, gate_row=2)


def _odd_layer(x2, mod, norm_g, w_in, w_out, pos_col, batch, seq):
    d = x2.shape[1]
    n_heads = d // HEAD_DIM
    qw = n_heads * HEAD_DIM
    iw = IDX_HEADS * IDX_DIM
    o_k, o_v, o_qi, o_ki, o_wi = qw, qw + HEAD_DIM, qw + 2 * HEAD_DIM, qw + 2 * HEAD_DIM + iw, qw + 2 * HEAD_DIM + iw + IDX_DIM
    w_q = w_in[:, :qw].astype(BF16)
    w_qi = w_in[:, o_qi:o_ki].astype(BF16)
    w_kvi = jnp.concatenate([w_in[:, o_k:o_qi], w_in[:, o_ki:],
                             jnp.zeros((d, LANES - IDX_DIM - IDX_HEADS), w_in.dtype)], axis=1).astype(BF16)

    cos, sin = _rope_tables(pos_col, HEAD_DIM // 2)
    cosi, sini = _rope_tables(pos_col, IDX_DIM // 2)
    q, qi, k, v, kia, kib, wi = _odd_projection(
        x2, norm_g[0], mod, jnp.concatenate([w_q, w_qi], axis=1), w_kvi, cos, sin, cosi, sini, seq,
        q_width=qw, q_scale=HEAD_DIM ** -0.5 * LOG2E)
    o = _dsa_attention(q, qi, wi, k, v, kia, kib, batch, seq, n_heads)
    return _outproj_residual([o], [w_out.astype(BF16)], x2, norm_g[1], mod, seq, gate_row=2)


def kernel(x, c, positions, ada_w, ada_b, norm_g, mix_w_out, even_w_in, even_b_forget, odd_w_in, ff_w1, ff_w2):
    batch, seq, d = x.shape
    depth = ada_w.shape[0]
    assert d % HEAD_DIM == 0 and seq % Q_BLOCK == 0 and seq >= TOPK_MAX
    mods = _ada_mod(c, ada_w, ada_b).reshape(depth, batch, 6, d)
    pos_col = positions.reshape(batch * seq, 1)
    x2 = x.reshape(batch * seq, d)
    for l in range(depth):
        mod = mods[l]
        if l % 2 == 0:
            x2 = _even_layer(x2, mod, norm_g[l], even_w_in[l // 2], even_b_forget[l // 2],
                             mix_w_out[l], batch, seq)
        else:
            x2 = _odd_layer(x2, mod, norm_g[l], odd_w_in[l // 2], mix_w_out[l], pos_col, batch, seq)
        x2 = _ffn_residual(x2, norm_g[l, 2], norm_g[l, 3], mod,
                           ff_w1[l].astype(BF16), ff_w2[l].astype(BF16), seq)
    return x2.reshape(batch, seq, d)
```

```python
import functools

import jax
import jax.numpy as jnp
from jax import lax
from jax.experimental import pallas as pl
from jax.experimental.pallas import tpu as pltpu

F32 = jnp.float32
BF16 = jnp.bfloat16
I32 = jnp.int32

HEAD_DIM = 128
CHUNK = 64
CHUNK_SHIFT = 6
Q_BLOCK = 128
IDX_HEADS = 16
IDX_DIM = 64
TOPK_MAX = 256
ROPE_THETA = 10000.0
EPS = 1e-6

LANES = 128
SUBLANES = 8
COUNT_ROWS = 4 * SUBLANES
ROW_BLOCK = 2 * SUBLANES
ROW_UNROLL = 8
INT_MIN = -(2 ** 31)
LOWEST_FINITE_KEY = INT_MIN + 2 ** 23
MASK_BIAS = -1e30
SB_EXIT_LOG2 = -152.0
LOG2E = 1.4426950408889634
VMEM_LIMIT_BYTES = 56 * 1024 * 1024


def _params(n_axes, vmem=None):
    kw = dict(dimension_semantics=("arbitrary",) * n_axes)
    if vmem is not None:
        kw["vmem_limit_bytes"] = vmem
    return pltpu.CompilerParams(**kw)


def _dot_nt(a, b):
    return lax.dot_general(a, b, (((1,), (1,)), ((), ())), preferred_element_type=F32)


def _split_bf16(x, parts):
    out = []
    r = x
    for _ in range(parts):
        p = r.astype(BF16)
        out.append(p)
        r = r - p.astype(F32)
    return out


def _log_sigmoid(x):
    return jnp.minimum(x, 0.0) - jnp.log1p(jnp.exp(-jnp.abs(x)))


def _ada_kernel(c_ref, w_ref, b_ref, o_ref):
    c = c_ref[...]
    cs = c / (1.0 + jnp.exp(-c))
    o_ref[0] = jnp.dot(cs, w_ref[0], preferred_element_type=F32,
                       precision=lax.Precision.HIGHEST) + b_ref[0]


def _ada_mod(c, ada_w, ada_b):
    depth, d, n = ada_w.shape
    b = c.shape[0]
    tn = min(1024, n)
    return pl.pallas_call(
        _ada_kernel,
        grid=(depth, n // tn),
        in_specs=[pl.BlockSpec((b, d), lambda l, j: (0, 0)),
                  pl.BlockSpec((1, d, tn), lambda l, j: (l, 0, j)),
                  pl.BlockSpec((1, 1, tn), lambda l, j: (l, 0, j))],
        out_specs=pl.BlockSpec((1, b, tn), lambda l, j: (l, 0, j)),
        out_shape=jax.ShapeDtypeStruct((depth, b, n), F32),
        compiler_params=_params(2, VMEM_LIMIT_BYTES),
        name="ada_mod",
    )(c, ada_w, ada_b.reshape(depth, 1, n))


def _rms(x, g):
    ms = jnp.mean(x * x, axis=-1, keepdims=True)
    return x * lax.rsqrt(ms + EPS) * g


def _for_row_blocks(n_rows, fn):
    def body(r, carry):
        fn(pl.ds(pl.multiple_of(r * ROW_BLOCK, ROW_BLOCK), ROW_BLOCK))
        return carry
    lax.fori_loop(0, n_rows // ROW_BLOCK, body, 0, unroll=ROW_UNROLL)


def _modulated_norm(x, g, mod_ref, sh_row, sc_row):
    return _rms(x, g) * (1.0 + mod_ref[0, sc_row:sc_row + 1, :]) + mod_ref[0, sh_row:sh_row + 1, :]


def _modnorm_into(h_ref, x_ref, g_ref, mod_ref, sh_row, sc_row):
    gain = g_ref[...] * (1.0 + mod_ref[0, sc_row:sc_row + 1, :])
    shift = mod_ref[0, sh_row:sh_row + 1, :]

    def rows_fn(rows):
        h_ref[rows, :] = (_rms(x_ref[rows, :], gain) + shift).astype(h_ref.dtype)
    _for_row_blocks(x_ref.shape[0], rows_fn)


def _gated_residual_into(out_ref, x_ref, y_ref, g_ref, mod_ref, gate_row):
    gain = g_ref[...] * mod_ref[0, gate_row:gate_row + 1, :]

    def rows_fn(rows):
        out_ref[rows, :] = x_ref[rows, :] + _rms(y_ref[rows, :], gain)
    _for_row_blocks(x_ref.shape[0], rows_fn)


def _rope128(a, cos, sin_signed):
    return a * cos + pltpu.roll(a, HEAD_DIM // 2, 1) * sin_signed


def _rope64(a, cos, sin_signed):
    lane = lax.broadcasted_iota(I32, a.shape, 1)
    first_half = (lane & (IDX_DIM - 1)) < (IDX_DIM // 2)
    rot = jnp.where(first_half, pltpu.roll(a, LANES - IDX_DIM // 2, 1), pltpu.roll(a, IDX_DIM // 2, 1))
    return a * cos + rot * sin_signed


def _even_proj_kernel(x_ref, g_ref, mod_ref, w_ref, wg_ref, o_ref, fg_ref, h_ref):
    @pl.when(pl.program_id(1) == 0)
    def _():
        h = _modulated_norm(x_ref[...], g_ref[...], mod_ref, 0, 1).astype(h_ref.dtype)
        h_ref[...] = h
        fg_ref[...] = jnp.dot(h, wg_ref[...], preferred_element_type=F32)

    o_ref[...] = jnp.dot(h_ref[...], w_ref[...], preferred_element_type=F32).astype(o_ref.dtype)


def _even_projection(x2, g, mod, w_main, w_gate, seq, tm=1024, tn=512):
    t, d = x2.shape
    n = w_main.shape[1]
    tm = min(tm, seq)
    tn = min(tn, n)
    nsb = seq // tm
    return pl.pallas_call(
        _even_proj_kernel,
        grid=(t // tm, n // tn),
        in_specs=[pl.BlockSpec((tm, d), lambda i, j: (i, 0)),
                  pl.BlockSpec((1, d), lambda i, j: (0, 0)),
                  pl.BlockSpec((1, 6, d), lambda i, j: (i // nsb, 0, 0)),
                  pl.BlockSpec((d, tn), lambda i, j: (0, j)),
                  pl.BlockSpec((d, LANES), lambda i, j: (0, 0))],
        out_specs=[pl.BlockSpec((tm, tn), lambda i, j: (i, j)),
                   pl.BlockSpec((tm, LANES), lambda i, j: (i, 0))],
        out_shape=[jax.ShapeDtypeStruct((t, n), BF16), jax.ShapeDtypeStruct((t, LANES), F32)],
        scratch_shapes=[pltpu.VMEM((tm, d), BF16)],
        compiler_params=_params(2, VMEM_LIMIT_BYTES),
        name="even_in_proj",
    )(x2, g.reshape(1, d), mod, w_main, w_gate)


def _odd_proj_kernel(x_ref, g_ref, mod_ref, w_ref, wkvi_ref, cos_ref, sin_ref, cosi_ref, sini_ref,
                     q_ref, qi_ref, k_ref, v_ref, kia_ref, kib_ref, wi_ref, h_ref, *, n_q_tiles, q_scale):
    j = pl.program_id(1)

    @pl.when(j == 0)
    def _():
        h = _modulated_norm(x_ref[...], g_ref[...], mod_ref, 0, 1).astype(h_ref.dtype)
        h_ref[...] = h
        acc = jnp.dot(h, wkvi_ref[...], preferred_element_type=F32)
        k_ref[...] = _rope128(acc[:, :LANES], cos_ref[...], sin_ref[...]).astype(k_ref.dtype)
        v_ref[:, :LANES] = acc[:, LANES:2 * LANES].astype(v_ref.dtype)
        v_ref[:, LANES:] = jnp.ones((acc.shape[0], LANES), v_ref.dtype)
        t3 = acc[:, 2 * LANES:]
        lane = lax.broadcasted_iota(I32, t3.shape, 1)
        ki = jnp.where(lane < IDX_DIM, _rope64(t3, cosi_ref[...], sini_ref[...]), 0.0)
        kia_ref[...] = ki.astype(kia_ref.dtype)
        kib_ref[...] = pltpu.roll(ki, IDX_DIM, 1).astype(kib_ref.dtype)
        wi_ref[...] = pltpu.roll(t3, IDX_DIM, 1) * (IDX_HEADS ** -0.5 * IDX_DIM ** -0.5)

    acc = jnp.dot(h_ref[...], w_ref[...], preferred_element_type=F32)
    lane_tiles = [slice(t * LANES, (t + 1) * LANES) for t in range(acc.shape[1] // LANES)]

    @pl.when(j < n_q_tiles)
    def _():
        cos = cos_ref[...] * q_scale
        sin = sin_ref[...] * q_scale
        for sl in lane_tiles:
            q_ref[:, sl] = _rope128(acc[:, sl], cos, sin).astype(q_ref.dtype)

    @pl.when(j >= n_q_tiles)
    def _():
        for sl in lane_tiles:
            qi_ref[:, sl] = _rope64(acc[:, sl], cosi_ref[...], sini_ref[...]).astype(qi_ref.dtype)


def _odd_projection(x2, g, mod, w_qqi, w_kvi, cos, sin, cosi, sini, seq, q_width, q_scale, tm=1024, tn=512):
    t, d = x2.shape
    n = w_qqi.shape[1]
    tm = min(tm, seq)
    tn = min(tn, q_width)
    nsb = seq // tm
    n_q_tiles = q_width // tn
    tab = pl.BlockSpec((tm, LANES), lambda i, j: (i, 0))
    shp = lambda w, dt: jax.ShapeDtypeStruct((t, w), dt)
    return pl.pallas_call(
        functools.partial(_odd_proj_kernel, n_q_tiles=n_q_tiles, q_scale=q_scale),
        grid=(t // tm, n // tn),
        in_specs=[pl.BlockSpec((tm, d), lambda i, j: (i, 0)),
                  pl.BlockSpec((1, d), lambda i, j: (0, 0)),
                  pl.BlockSpec((1, 6, d), lambda i, j: (i // nsb, 0, 0)),
                  pl.BlockSpec((d, tn), lambda i, j: (0, j)),
                  pl.BlockSpec(w_kvi.shape, lambda i, j: (0, 0)), tab, tab, tab, tab],
        out_specs=[pl.BlockSpec((tm, tn), lambda i, j: (i, jnp.minimum(j, n_q_tiles - 1))),
                   pl.BlockSpec((tm, tn), lambda i, j: (i, jnp.maximum(j - n_q_tiles, 0))),
                   tab, pl.BlockSpec((tm, 2 * LANES), lambda i, j: (i, 0)), tab, tab, tab],
        out_shape=[shp(q_width, BF16), shp(n - q_width, BF16), shp(LANES, BF16), shp(2 * LANES, BF16),
                   shp(LANES, BF16), shp(LANES, BF16), shp(LANES, F32)],
        scratch_shapes=[pltpu.VMEM((tm, d), BF16)],
        compiler_params=_params(2, VMEM_LIMIT_BYTES),
        name="odd_in_proj",
    )(x2, g.reshape(1, d), mod, w_qqi, w_kvi, cos, sin, cosi, sini)


def _rope_tab_kernel(pos_ref, inv_ref, sgn_ref, cos_ref, sin_ref):
    ang = pos_ref[...].astype(F32) * inv_ref[...]
    cos_ref[...] = jnp.cos(ang)
    sin_ref[...] = jnp.sin(ang) * sgn_ref[...]


def _rope_tables(pos_col, half):
    t = pos_col.shape[0]
    inv = ROPE_THETA ** (-jnp.arange(half, dtype=F32) / half)
    reps = LANES // (2 * half)
    inv_row = jnp.tile(jnp.concatenate([inv, inv]), reps).reshape(1, LANES)
    sgn_row = jnp.tile(jnp.concatenate([-jnp.ones(half, F32), jnp.ones(half, F32)]), reps).reshape(1, LANES)
    tm = min(1024, t)
    row = pl.BlockSpec((1, LANES), lambda i: (0, 0))
    tab = pl.BlockSpec((tm, LANES), lambda i: (i, 0))
    return pl.pallas_call(
        _rope_tab_kernel,
        grid=(t // tm,),
        in_specs=[pl.BlockSpec((tm, 1), lambda i: (i, 0)), row, row],
        out_specs=[tab, tab],
        out_shape=[jax.ShapeDtypeStruct((t, LANES), F32)] * 2,
        compiler_params=_params(1),
        name="rope_tables",
    )(pos_col, inv_row, sgn_row)


def _gate_cumsum_kernel(fg_ref, b_ref, f_ref, carry_ref):
    @pl.when(pl.program_id(1) == 0)
    def _():
        carry_ref[...] = jnp.zeros_like(carry_ref)

    lf = _log_sigmoid(fg_ref[...] + b_ref[...])
    tc = lf.shape[0]
    r = lax.broadcasted_iota(I32, (tc, tc), 0)
    c = lax.broadcasted_iota(I32, (tc, tc), 1)
    tri = (c <= r).astype(BF16)
    cs = carry_ref[...]
    for piece in _split_bf16(lf, 3):
        cs = cs + jnp.dot(tri, piece, preferred_element_type=F32)
    f_ref[...] = cs * LOG2E
    carry_ref[...] = cs[tc - 1:tc, :]


def _gate_cumsum(fg, b_row, batch, seq):
    tc = min(256, seq)
    nsb = seq // tc
    return pl.pallas_call(
        _gate_cumsum_kernel,
        grid=(batch, nsb),
        in_specs=[pl.BlockSpec((tc, LANES), lambda b, j: (b * nsb + j, 0)),
                  pl.BlockSpec((1, LANES), lambda b, j: (0, 0))],
        out_specs=pl.BlockSpec((tc, LANES), lambda b, j: (b * nsb + j, 0)),
        out_shape=jax.ShapeDtypeStruct(fg.shape, F32),
        scratch_shapes=[pltpu.VMEM((1, LANES), F32)],
        compiler_params=_params(2),
        name="gate_cumsum",
    )(fg, b_row)


def _softmax_block(tiles, m_prev):
    mx = functools.reduce(jnp.maximum, tiles)
    m_new = jnp.maximum(m_prev, jnp.broadcast_to(jnp.max(mx, axis=-1, keepdims=True), mx.shape))
    alpha = jnp.exp2(m_prev - m_new)
    return m_new, alpha, [jnp.exp2(t - m_new) for t in tiles]


def _fox_kernel(q_ref, k_ref, v_ref, f_ref, o_ref, m_ref, l_ref, acc_ref, sa_ref, sb_ref, *, tq):
    i = pl.program_id(2)
    m_ref[...] = jnp.full_like(m_ref, -jnp.inf)
    l_ref[...] = jnp.zeros_like(l_ref)
    acc_ref[...] = jnp.zeros_like(acc_ref)
    n_tiles = tq // LANES

    def keys(kj):
        return pl.ds(pl.multiple_of(kj * tq, tq), tq)

    def scores(kj, s_ref):
        s_ref[...] = _dot_nt(q_ref[...], k_ref[keys(kj), :])

    def softmax_pv(kj, s_ref, masked):
        ks = keys(kj)
        s = s_ref[...] - f_ref[0, :, ks]
        if masked:
            row = lax.broadcasted_iota(I32, s.shape, 0)
            col = lax.broadcasted_iota(I32, s.shape, 1)
            s = jnp.where(col <= row, s, -jnp.inf)
        tiles = [s[:, t * LANES:(t + 1) * LANES] for t in range(n_tiles)]
        m_new, alpha, p = _softmax_block(tiles, m_ref[...])
        psum = functools.reduce(jnp.add, p)
        l_ref[...] = alpha * l_ref[...] + jnp.broadcast_to(jnp.sum(psum, axis=-1, keepdims=True), psum.shape)
        pv = jnp.dot(jnp.concatenate(p, axis=-1).astype(BF16), v_ref[ks, :], preferred_element_type=F32)
        acc_ref[...] = alpha * acc_ref[...] + pv
        m_ref[...] = m_new

    scores(0, sa_ref)

    def pair(p, carry):
        scores(2 * p + 1, sb_ref)
        softmax_pv(2 * p, sa_ref, masked=False)
        scores(2 * p + 2, sa_ref)
        softmax_pv(2 * p + 1, sb_ref, masked=False)
        return carry

    lax.fori_loop(0, i // 2, pair, 0)

    @pl.when(i % 2 == 0)
    def _():
        softmax_pv(i, sa_ref, masked=True)

    @pl.when(i % 2 == 1)
    def _():
        scores(i, sb_ref)
        softmax_pv(i - 1, sa_ref, masked=False)
        softmax_pv(i, sb_ref, masked=True)

    o_ref[...] = (acc_ref[...] / l_ref[...]).astype(o_ref.dtype)


def _fox_attention(qkv, f_rows, batch, seq, n_heads, q_col, k_col, v_col):
    tq = min(512, seq)
    nq = seq // tq
    t = qkv.shape[0]
    kernel = functools.partial(_fox_kernel, tq=tq)
    return pl.pallas_call(
        kernel,
        grid=(batch, n_heads, nq),
        in_specs=[pl.BlockSpec((tq, HEAD_DIM), lambda b, h, i: (b * nq + i, q_col + h)),
                  pl.BlockSpec((seq, HEAD_DIM), lambda b, h, i: (b, k_col + h)),
                  pl.BlockSpec((seq, HEAD_DIM), lambda b, h, i: (b, v_col + h)),
                  pl.BlockSpec((1, 1, seq), lambda b, h, i: (b * n_heads + h, 0, 0))],
        out_specs=pl.BlockSpec((tq, HEAD_DIM), lambda b, h, i: (b * nq + i, h)),
        out_shape=jax.ShapeDtypeStruct((t, n_heads * HEAD_DIM), BF16),
        scratch_shapes=[pltpu.VMEM((tq, LANES), F32), pltpu.VMEM((tq, LANES), F32),
                        pltpu.VMEM((tq, HEAD_DIM), F32),
                        pltpu.VMEM((tq, tq), F32), pltpu.VMEM((tq, tq), F32)],
        compiler_params=_params(3, VMEM_LIMIT_BYTES),
        name="fox_attention",
    )(qkv, qkv, qkv, f_rows)


def _sb_kernel(q_ref, k_ref, v_ref, o_ref, c_ref, acc_ref, *, tq, tk, hg):
    i = pl.program_id(2)
    c_ref[...] = jnp.zeros_like(c_ref)
    acc_ref[...] = jnp.zeros_like(acc_ref)
    r = lax.broadcasted_iota(I32, (2 * tk, tk), 0)
    cc = lax.broadcasted_iota(I32, (2 * tk, tk), 1)
    upper2 = ((r & (tk - 1)) > cc).astype(BF16)
    nblk = (i + 1) * (tq // tk)
    col_minus_row = lax.broadcasted_iota(I32, (tq, tk), 1) - lax.broadcasted_iota(I32, (tq, tk), 0)

    def cond(carry):
        step, c_max = carry
        return jnp.logical_and(step < nblk, c_max > SB_EXIT_LOG2)

    def body(carry):
        step, _ = carry
        kj = nblk - 1 - step
        ks = pl.ds(pl.multiple_of(kj * tk, tk), tk)
        strict = col_minus_row < (i * tq - kj * tk)
        c_max = None
        for g in range(hg):
            hd = slice(g * HEAD_DIM, (g + 1) * HEAD_DIM)
            z = _dot_nt(q_ref[:, hd], k_ref[ks, hd])
            ls = jnp.minimum(z, 0.0) - jnp.log2(1.0 + jnp.exp2(-jnp.abs(z)))
            lneg = jnp.where(strict, ls - z, 0.0)
            suffix = jnp.dot(jnp.concatenate(_split_bf16(lneg, 2), axis=1), upper2, preferred_element_type=F32)
            c_prev = c_ref[g]
            a = jnp.where(strict, jnp.exp2(ls + suffix + c_prev), 0.0)
            acc_ref[:, hd] += jnp.dot(a.astype(BF16), v_ref[ks, hd], preferred_element_type=F32)
            c_new = c_prev + suffix[:, 0:1] + lneg[:, 0:1]
            c_ref[g] = c_new
            c_max = jnp.max(c_new) if c_max is None else jnp.maximum(c_max, jnp.max(c_new))
        return step + 1, c_max

    lax.while_loop(cond, body, (jnp.int32(0), jnp.float32(0.0)))
    o_ref[...] = acc_ref[...].astype(o_ref.dtype)


def _sb_attention(qkv, batch, seq, n_heads, q_col, k_col, v_col):
    tq = min(512, seq)
    tk = min(256, seq)
    nq = seq // tq
    t = qkv.shape[0]
    hg = 2 if (n_heads % 2 == 0 and q_col % 2 == 0 and k_col % 2 == 0 and v_col % 2 == 0) else 1
    w = hg * HEAD_DIM
    kernel = functools.partial(_sb_kernel, tq=tq, tk=tk, hg=hg)
    return pl.pallas_call(
        kernel,
        grid=(batch, n_heads // hg, nq),
        in_specs=[pl.BlockSpec((tq, w), lambda b, h, i: (b * nq + i, q_col // hg + h)),
                  pl.BlockSpec((seq, w), lambda b, h, i: (b, k_col // hg + h)),
                  pl.BlockSpec((seq, w), lambda b, h, i: (b, v_col // hg + h))],
        out_specs=pl.BlockSpec((tq, w), lambda b, h, i: (b * nq + i, h)),
        out_shape=jax.ShapeDtypeStruct((t, n_heads * HEAD_DIM), BF16),
        scratch_shapes=[pltpu.VMEM((hg, tq, 1), F32), pltpu.VMEM((tq, w), F32)],
        compiler_params=_params(3, VMEM_LIMIT_BYTES),
        name="sb_attention",
    )(qkv, qkv, qkv)


def _dsa_kernel(q_ref, qi_ref, wi_ref, k_ref, v_ref, kia_ref, kib_ref, o_ref,
                keys_ref, keyst_ref, qs_ref, m_ref, acc_ref, sa_ref, sb_ref, *, n_heads, kc, kca, topk):
    i = pl.program_id(1)
    tq = Q_BLOCK
    nch = ((i + 1) * tq + kc - 1) // kc
    row_t = lax.broadcasted_iota(I32, (tq, kc), 0) + i * tq
    col_l = lax.broadcasted_iota(I32, (tq, kc), 1)

    def index_body(c, carry):
        ks = pl.ds(pl.multiple_of(c * kc, kc), kc)
        kia = kia_ref[ks, :]
        kib = kib_ref[ks, :]
        wi = wi_ref[...]
        score = jnp.zeros((tq, kc), F32)
        for p in range(IDX_HEADS // 2):
            qp = qi_ref[:, p * LANES:(p + 1) * LANES]
            score = score + jnp.maximum(_dot_nt(qp, kia), 0.0) * wi[:, 2 * p:2 * p + 1]
            score = score + jnp.maximum(_dot_nt(qp, kib), 0.0) * wi[:, 2 * p + 1:2 * p + 2]
        adm = ((col_l + c * kc) >> CHUNK_SHIFT) <= (row_t >> CHUNK_SHIFT)
        score = jnp.where(adm, score + 0.0, -jnp.inf)
        keys_ref[:, ks] = score
        keyst_ref[ks, :] = score.T
        return carry

    lax.fori_loop(0, nch, index_body, 0)

    def key_to_float(key):
        return lax.bitcast_convert_type(jnp.where(key >= 0, key, key ^ 0x7FFFFFFF), F32)

    def count_ge(trial_key):
        trial = key_to_float(trial_key)

        def body(c, cnt):
            kk = keyst_ref[pl.ds(pl.multiple_of(c * kc, kc), kc), :]
            hit = jnp.where(kk >= trial, 1.0, 0.0)
            return cnt + jnp.sum(hit.reshape(kc // COUNT_ROWS, COUNT_ROWS, tq), axis=0)
        cnt = lax.fori_loop(0, nch, body, jnp.zeros((COUNT_ROWS, tq), F32))
        return jnp.sum(cnt, axis=0, keepdims=True)

    def per_row(v):
        return jnp.broadcast_to(v, (LANES, tq)).T

    kf = float(topk)
    cur_q = jnp.where(count_ge(jnp.zeros((1, tq), I32)) >= kf, 0, INT_MIN).astype(I32)

    def search_body(it, cur):
        trial = cur + jnp.left_shift(jnp.int32(1), 30 - it)
        return jnp.where(count_ge(trial) >= kf, trial, cur)

    cur_q = lax.fori_loop(0, 31, search_body, cur_q)
    cur = per_row(key_to_float(cur_q))
    thr = per_row(key_to_float(jnp.maximum(cur_q, LOWEST_FINITE_KEY)))

    surplus = jnp.where((count_ge(cur_q) > kf) & (cur_q > INT_MIN), 1.0, 0.0)

    @pl.when(jnp.max(surplus) > 0.0)
    def _():
        need = per_row(kf - count_ge(cur_q + 1))
        r = lax.broadcasted_iota(I32, (kc, kc), 0)
        cc = lax.broadcasted_iota(I32, (kc, kc), 1)
        before = (r < cc).astype(BF16)
        lane_tiles = [slice(t * LANES, (t + 1) * LANES) for t in range(kc // LANES)]

        def tie_body(c, run):
            base = pl.multiple_of(c * kc, kc)
            kk = keys_ref[:, pl.ds(base, kc)]
            eq = [kk[:, sl] == cur for sl in lane_tiles]
            eqf = [jnp.where(e, 1.0, 0.0) for e in eq]
            rank = jnp.dot(jnp.concatenate(eqf, axis=-1).astype(BF16), before, preferred_element_type=F32)
            for t, sl in enumerate(lane_tiles):
                retire = eq[t] & (rank[:, sl] + run >= need)
                keys_ref[:, pl.ds(base + t * LANES, LANES)] = jnp.where(retire, -jnp.inf, kk[:, sl])
            tot = functools.reduce(jnp.add, eqf)
            return run + jnp.broadcast_to(jnp.sum(tot, axis=-1, keepdims=True), tot.shape)

        lax.fori_loop(0, nch, tie_body, jnp.zeros((tq, LANES), F32))

    for h in range(n_heads):
        qs_ref[h * tq:(h + 1) * tq, :] = q_ref[:, h * HEAD_DIM:(h + 1) * HEAD_DIM]
    m_ref[...] = jnp.full_like(m_ref, MASK_BIAS)
    acc_ref[...] = jnp.zeros_like(acc_ref)
    n_att = ((i + 1) * tq + kca - 1) // kca
    n_tiles = kca // LANES
    scored = nch * kc

    def chunk(c):
        return pl.ds(pl.multiple_of(c * kca, kca), kca)

    def scores(c, s_ref):
        s_ref[...] = _dot_nt(qs_ref[...], k_ref[chunk(jnp.minimum(c, n_att - 1)), :])

    def softmax_pv(c, s_ref):
        ks = chunk(c)
        lane = lax.broadcasted_iota(I32, (tq, LANES), 1)
        kk = keys_ref[:, ks]
        s = s_ref[...].reshape(n_heads, tq, kca)
        tiles = []
        for t in range(n_tiles):
            sl = slice(t * LANES, (t + 1) * LANES)
            bias = jnp.where((kk[:, sl] >= thr) & (lane < scored - c * kca - t * LANES), 0.0, MASK_BIAS)
            tiles.append(s[:, :, sl] + bias[None])
        m_new, alpha, p = _softmax_block(tiles, m_ref[...])
        pb = jnp.concatenate(p, axis=-1).reshape(n_heads * tq, kca).astype(BF16)
        pv = jnp.dot(pb, v_ref[ks, :], preferred_element_type=F32)
        a2 = alpha.reshape(n_heads * tq, LANES)
        acc_ref[...] = jnp.concatenate([a2, a2], axis=-1) * acc_ref[...] + pv
        m_ref[...] = m_new

    scores(0, sa_ref)

    def pair(p, carry):
        scores(2 * p + 1, sb_ref)
        softmax_pv(2 * p, sa_ref)
        scores(2 * p + 2, sa_ref)
        softmax_pv(2 * p + 1, sb_ref)
        return carry

    lax.fori_loop(0, n_att // 2, pair, 0)

    @pl.when(n_att % 2 == 1)
    def _():
        softmax_pv(n_att - 1, sa_ref)

    acc = acc_ref[...]
    out = acc[:, :HEAD_DIM] / acc[:, HEAD_DIM:]
    for h in range(n_heads):
        o_ref[:, h * HEAD_DIM:(h + 1) * HEAD_DIM] = out[h * tq:(h + 1) * tq, :].astype(o_ref.dtype)


def _dsa_attention(q, qi, wi, k, v, kia, kib, batch, seq, n_heads):
    tq = Q_BLOCK
    kc = min(256, seq)
    nq = seq // tq
    topk = min(TOPK_MAX, seq // 4)
    t = q.shape[0]
    kca = min(512, seq)
    kernel = functools.partial(_dsa_kernel, n_heads=n_heads, kc=kc, kca=kca, topk=topk)
    qblk = lambda w: pl.BlockSpec((tq, w), lambda b, i: (b * nq + i, 0))
    full = pl.BlockSpec((seq, LANES), lambda b, i: (b, 0))
    vext = pl.BlockSpec((seq, 2 * LANES), lambda b, i: (b, 0))
    return pl.pallas_call(
        kernel,
        grid=(batch, nq),
        in_specs=[qblk(n_heads * HEAD_DIM), qblk(IDX_HEADS * IDX_DIM), qblk(LANES), full, vext, full, full],
        out_specs=qblk(n_heads * HEAD_DIM),
        out_shape=jax.ShapeDtypeStruct((t, n_heads * HEAD_DIM), BF16),
        scratch_shapes=[pltpu.VMEM((tq, seq), F32),
                        pltpu.VMEM((seq, tq), F32),
                        pltpu.VMEM((n_heads * tq, HEAD_DIM), BF16),
                        pltpu.VMEM((n_heads, tq, LANES), F32),
                        pltpu.VMEM((n_heads * tq, 2 * LANES), F32),
                        pltpu.VMEM((n_heads * tq, kca), F32),
                        pltpu.VMEM((n_heads * tq, kca), F32)],
        compiler_params=_params(2, VMEM_LIMIT_BYTES),
        name="dsa_attention",
    )(q, qi, wi, k, v, kia, kib)


def _outproj_kernel(*refs, n_parts, gate_row):
    o_refs = refs[:n_parts]
    w_refs = refs[n_parts:2 * n_parts]
    x_ref, g_ref, mod_ref, out_ref = refs[2 * n_parts:]
    y = jnp.dot(o_refs[0][...], w_refs[0][...], preferred_element_type=F32)
    for o_r, w_r in zip(o_refs[1:], w_refs[1:]):
        y = y + jnp.dot(o_r[...], w_r[...], preferred_element_type=F32)
    out_ref[...] = x_ref[...] + mod_ref[0, gate_row:gate_row + 1, :] * _rms(y, g_ref[...])


def _outproj_residual(o_parts, w_parts, x2, g, mod, seq, gate_row):
    t, d = x2.shape
    tm = min(512, seq)
    nsb = seq // tm
    n_parts = len(o_parts)
    in_specs = [pl.BlockSpec((tm, o.shape[1]), lambda i: (i, 0)) for o in o_parts]
    in_specs += [pl.BlockSpec(w.shape, lambda i: (0, 0)) for w in w_parts]
    in_specs += [pl.BlockSpec((tm, d), lambda i: (i, 0)),
                 pl.BlockSpec((1, d), lambda i: (0, 0)),
                 pl.BlockSpec((1, 6, d), lambda i: (i // nsb, 0, 0))]
    return pl.pallas_call(
        functools.partial(_outproj_kernel, n_parts=n_parts, gate_row=gate_row),
        grid=(t // tm,),
        in_specs=in_specs,
        out_specs=pl.BlockSpec((tm, d), lambda i: (i, 0)),
        out_shape=jax.ShapeDtypeStruct((t, d), F32),
        compiler_params=_params(1, VMEM_LIMIT_BYTES),
        name="outproj_residual",
    )(*o_parts, *w_parts, x2, g.reshape(1, d), mod)


def _ffn_kernel(x_ref, g_in_ref, g_out_ref, mod_ref, w1_ref, w2_ref, out_ref, h_ref, acc_ref):
    j = pl.program_id(1)

    @pl.when(j == 0)
    def _():
        _modnorm_into(h_ref, x_ref, g_in_ref, mod_ref, 3, 4)
        acc_ref[...] = jnp.zeros_like(acc_ref)

    u = jnp.maximum(jnp.dot(h_ref[...], w1_ref[...], preferred_element_type=F32), 0.0)
    acc_ref[...] += jnp.dot((u * u).astype(BF16), w2_ref[...], preferred_element_type=F32)

    @pl.when(j == pl.num_programs(1) - 1)
    def _():
        _gated_residual_into(out_ref, x_ref, acc_ref, g_out_ref, mod_ref, 5)


def _ffn_residual(x2, g_in, g_out, mod, w1, w2, seq):
    t, d = x2.shape
    f = w1.shape[1]
    tm = min(512, seq)
    tf = min(1024, f)
    nsb = seq // tm
    row = pl.BlockSpec((1, d), lambda i, j: (0, 0))
    return pl.pallas_call(
        _ffn_kernel,
        grid=(t // tm, f // tf),
        in_specs=[pl.BlockSpec((tm, d), lambda i, j: (i, 0)), row, row,
                  pl.BlockSpec((1, 6, d), lambda i, j: (i // nsb, 0, 0)),
                  pl.BlockSpec((d, tf), lambda i, j: (0, j)),
                  pl.BlockSpec((tf, d), lambda i, j: (j, 0))],
        out_specs=pl.BlockSpec((tm, d), lambda i, j: (i, 0)),
        out_shape=jax.ShapeDtypeStruct((t, d), F32),
        scratch_shapes=[pltpu.VMEM((tm, d), BF16), pltpu.VMEM((tm, d), F32)],
        compiler_params=_params(2, VMEM_LIMIT_BYTES),
        name="ffn_residual",
    )(x2, g_in.reshape(1, d), g_out.reshape(1, d), mod, w1, w2)


def _even_layer(x2, mod, norm_g, w_in, b_forget, w_out, batch, seq):
    d = x2.shape[1]
    n_heads = d // HEAD_DIM
    n_fox = n_heads // 2
    n_sb = n_heads - n_fox
    fw = n_fox * HEAD_DIM
    sw = n_sb * HEAD_DIM
    scale = HEAD_DIM ** -0.5
    w_main = jnp.concatenate([w_in[:, :fw] * (scale * LOG2E), w_in[:, fw:3 * fw],
                              w_in[:, 3 * fw + n_fox:3 * fw + n_fox + sw] * (scale * LOG2E),
                              w_in[:, 3 * fw + n_fox + sw:]], axis=1).astype(BF16)
    w_gate = jnp.pad(w_in[:, 3 * fw:3 * fw + n_fox], ((0, 0), (0, LANES - n_fox))).astype(BF16)
    b_row = jnp.pad(b_forget.astype(F32), (0, LANES - n_fox)).reshape(1, LANES)

    qkv, fg = _even_projection(x2, norm_g[0], mod, w_main, w_gate, seq)
    f_cum = _gate_cumsum(fg, b_row, batch, seq)
    f_rows = f_cum.reshape(batch, seq, LANES)[:, :, :n_fox].transpose(0, 2, 1).reshape(batch * n_fox, 1, seq)
    o_f = _fox_attention(qkv, f_rows, batch, seq, n_fox, 0, n_fox, 2 * n_fox)
    o_s = _sb_attention(qkv, batch, seq, n_sb, 3 * n_fox, 3 * n_fox + n_sb, 3 * n_fox + 2 * n_sb)
    w_o = w_out.astype(BF16)
    return _outproj_residual([o_f, o_s], [w_o[:fw], w_o[fw:]], x2, norm_g[1], mod, seq, gate_row=2)


def _odd_layer(x2, mod, norm_g, w_in, w_out, pos_col, batch, seq):
    d = x2.shape[1]
    n_heads = d // HEAD_DIM
    qw = n_heads * HEAD_DIM
    iw = IDX_HEADS * IDX_DIM
    o_k, o_v, o_qi, o_ki, o_wi = qw, qw + HEAD_DIM, qw + 2 * HEAD_DIM, qw + 2 * HEAD_DIM + iw, qw + 2 * HEAD_DIM + iw + IDX_DIM
    w_q = w_in[:, :qw].astype(BF16)
    w_qi = w_in[:, o_qi:o_ki].astype(BF16)
    w_kvi = jnp.concatenate([w_in[:, o_k:o_qi], w_in[:, o_ki:],
                             jnp.zeros((d, LANES - IDX_DIM - IDX_HEADS), w_in.dtype)], axis=1).astype(BF16)

    cos, sin = _rope_tables(pos_col, HEAD_DIM // 2)
    cosi, sini = _rope_tables(pos_col, IDX_DIM // 2)
    q, qi, k, v, kia, kib, wi = _odd_projection(
        x2, norm_g[0], mod, jnp.concatenate([w_q, w_qi], axis=1), w_kvi, cos, sin, cosi, sini, seq,
        q_width=qw, q_scale=HEAD_DIM ** -0.5 * LOG2E)
    o = _dsa_attention(q, qi, wi, k, v, kia, kib, batch, seq, n_heads)
    return _outproj_residual([o], [w_out.astype(BF16)], x2, norm_g[1], mod, seq, gate_row=2)


def kernel(x, c, positions, ada_w, ada_b, norm_g, mix_w_out, even_w_in, even_b_forget, odd_w_in, ff_w1, ff_w2):
    batch, seq, d = x.shape
    depth = ada_w.shape[0]
    assert d % HEAD_DIM == 0 and seq % Q_BLOCK == 0 and seq >= TOPK_MAX
    mods = _ada_mod(c, ada_w, ada_b).reshape(depth, batch, 6, d)
    pos_col = positions.reshape(batch * seq, 1)
    x2 = x.reshape(batch * seq, d)
    for l in range(depth):
        mod = mods[l]
        if l % 2 == 0:
            x2 = _even_layer(x2, mod, norm_g[l], even_w_in[l // 2], even_b_forget[l // 2],
                             mix_w_out[l], batch, seq)
        else:
            x2 = _odd_layer(x2, mod, norm_g[l], odd_w_in[l // 2], mix_w_out[l], pos_col, batch, seq)
        x2 = _ffn_residual(x2, norm_g[l, 2], norm_g[l, 3], mod,
                           ff_w1[l].astype(BF16), ff_w2[l].astype(BF16), seq)
    return x2.reshape(batch, seq, d)
```

```python
import functools

import jax
import jax.numpy as jnp
from jax import lax
from jax.experimental import pallas as pl
from jax.experimental.pallas import tpu as pltpu

F32 = jnp.float32
BF16 = jnp.bfloat16
I32 = jnp.int32

HEAD_DIM = 128
CHUNK = 64
CHUNK_SHIFT = 6
Q_BLOCK = 128
IDX_HEADS = 16
IDX_DIM = 64
TOPK_MAX = 256
ROPE_THETA = 10000.0
EPS = 1e-6

LANES = 128
SUBLANES = 8
COUNT_ROWS = 4 * SUBLANES
ROW_BLOCK = 2 * SUBLANES
ROW_UNROLL = 8
INT_MIN = -(2 ** 31)
LOWEST_FINITE_KEY = INT_MIN + 2 ** 23
MASK_BIAS = -1e30
SB_EXIT_LOG2 = -152.0
LOG2E = 1.4426950408889634
VMEM_LIMIT_BYTES = 56 * 1024 * 1024


def _params(n_axes, vmem=None):
    kw = dict(dimension_semantics=("arbitrary",) * n_axes)
    if vmem is not None:
        kw["vmem_limit_bytes"] = vmem
    return pltpu.CompilerParams(**kw)


def _dot_nt(a, b):
    return lax.dot_general(a, b, (((1,), (1,)), ((), ())), preferred_element_type=F32)


def _split_bf16(x, parts):
    out = []
    r = x
    for _ in range(parts):
        p = r.astype(BF16)
        out.append(p)
        r = r - p.astype(F32)
    return out


def _log_sigmoid(x):
    return jnp.minimum(x, 0.0) - jnp.log1p(jnp.exp(-jnp.abs(x)))


def _ada_kernel(c_ref, w_ref, b_ref, o_ref):
    c = c_ref[...]
    cs = c / (1.0 + jnp.exp(-c))
    o_ref[0] = jnp.dot(cs, w_ref[0], preferred_element_type=F32,
                       precision=lax.Precision.HIGHEST) + b_ref[0]


def _ada_mod(c, ada_w, ada_b):
    depth, d, n = ada_w.shape
    b = c.shape[0]
    tn = min(1024, n)
    return pl.pallas_call(
        _ada_kernel,
        grid=(depth, n // tn),
        in_specs=[pl.BlockSpec((b, d), lambda l, j: (0, 0)),
                  pl.BlockSpec((1, d, tn), lambda l, j: (l, 0, j)),
                  pl.BlockSpec((1, 1, tn), lambda l, j: (l, 0, j))],
        out_specs=pl.BlockSpec((1, b, tn), lambda l, j: (l, 0, j)),
        out_shape=jax.ShapeDtypeStruct((depth, b, n), F32),
        compiler_params=_params(2, VMEM_LIMIT_BYTES),
        name="ada_mod",
    )(c, ada_w, ada_b.reshape(depth, 1, n))


def _rms(x, g):
    ms = jnp.mean(x * x, axis=-1, keepdims=True)
    return x * lax.rsqrt(ms + EPS) * g


def _for_row_blocks(n_rows, fn):
    def body(r, carry):
        fn(pl.ds(pl.multiple_of(r * ROW_BLOCK, ROW_BLOCK), ROW_BLOCK))
        return carry
    lax.fori_loop(0, n_rows // ROW_BLOCK, body, 0, unroll=ROW_UNROLL)


def _modulated_norm(x, g, mod_ref, sh_row, sc_row):
    return _rms(x, g) * (1.0 + mod_ref[0, sc_row:sc_row + 1, :]) + mod_ref[0, sh_row:sh_row + 1, :]


def _modnorm_into(h_ref, x_ref, g_ref, mod_ref, sh_row, sc_row):
    gain = g_ref[...] * (1.0 + mod_ref[0, sc_row:sc_row + 1, :])
    shift = mod_ref[0, sh_row:sh_row + 1, :]

    def rows_fn(rows):
        h_ref[rows, :] = (_rms(x_ref[rows, :], gain) + shift).astype(h_ref.dtype)
    _for_row_blocks(x_ref.shape[0], rows_fn)


def _gated_residual_into(out_ref, x_ref, y_ref, g_ref, mod_ref, gate_row):
    gain = g_ref[...] * mod_ref[0, gate_row:gate_row + 1, :]

    def rows_fn(rows):
        out_ref[rows, :] = x_ref[rows, :] + _rms(y_ref[rows, :], gain)
    _for_row_blocks(x_ref.shape[0], rows_fn)


def _rope128(a, cos, sin_signed):
    return a * cos + pltpu.roll(a, HEAD_DIM // 2, 1) * sin_signed


def _rope64(a, cos, sin_signed):
    lane = lax.broadcasted_iota(I32, a.shape, 1)
    first_half = (lane & (IDX_DIM - 1)) < (IDX_DIM // 2)
    rot = jnp.where(first_half, pltpu.roll(a, LANES - IDX_DIM // 2, 1), pltpu.roll(a, IDX_DIM // 2, 1))
    return a * cos + rot * sin_signed


def _even_proj_kernel(x_ref, g_ref, mod_ref, w_ref, wg_ref, o_ref, fg_ref, h_ref):
    @pl.when(pl.program_id(1) == 0)
    def _():
        h = _modulated_norm(x_ref[...], g_ref[...], mod_ref, 0, 1).astype(h_ref.dtype)
        h_ref[...] = h
        fg_ref[...] = jnp.dot(h, wg_ref[...], preferred_element_type=F32)

    o_ref[...] = jnp.dot(h_ref[...], w_ref[...], preferred_element_type=F32).astype(o_ref.dtype)


def _even_projection(x2, g, mod, w_main, w_gate, seq, tm=1024, tn=512):
    t, d = x2.shape
    n = w_main.shape[1]
    tm = min(tm, seq)
    tn = min(tn, n)
    nsb = seq // tm
    return pl.pallas_call(
        _even_proj_kernel,
        grid=(t // tm, n // tn),
        in_specs=[pl.BlockSpec((tm, d), lambda i, j: (i, 0)),
                  pl.BlockSpec((1, d), lambda i, j: (0, 0)),
                  pl.BlockSpec((1, 6, d), lambda i, j: (i // nsb, 0, 0)),
                  pl.BlockSpec((d, tn), lambda i, j: (0, j)),
                  pl.BlockSpec((d, LANES), lambda i, j: (0, 0))],
        out_specs=[pl.BlockSpec((tm, tn), lambda i, j: (i, j)),
                   pl.BlockSpec((tm, LANES), lambda i, j: (i, 0))],
        out_shape=[jax.ShapeDtypeStruct((t, n), BF16), jax.ShapeDtypeStruct((t, LANES), F32)],
        scratch_shapes=[pltpu.VMEM((tm, d), BF16)],
        compiler_params=_params(2, VMEM_LIMIT_BYTES),
        name="even_in_proj",
    )(x2, g.reshape(1, d), mod, w_main, w_gate)


def _odd_proj_kernel(x_ref, g_ref, mod_ref, w_ref, wkvi_ref, cos_ref, sin_ref, cosi_ref, sini_ref,
                     q_ref, qi_ref, k_ref, v_ref, kia_ref, kib_ref, wi_ref, h_ref, *, n_q_tiles, q_scale):
    j = pl.program_id(1)

    @pl.when(j == 0)
    def _():
        h = _modulated_norm(x_ref[...], g_ref[...], mod_ref, 0, 1).astype(h_ref.dtype)
        h_ref[...] = h
        acc = jnp.dot(h, wkvi_ref[...], preferred_element_type=F32)
        k_ref[...] = _rope128(acc[:, :LANES], cos_ref[...], sin_ref[...]).astype(k_ref.dtype)
        v_ref[:, :LANES] = acc[:, LANES:2 * LANES].astype(v_ref.dtype)
        v_ref[:, LANES:] = jnp.ones((acc.shape[0], LANES), v_ref.dtype)
        t3 = acc[:, 2 * LANES:]
        lane = lax.broadcasted_iota(I32, t3.shape, 1)
        ki = jnp.where(lane < IDX_DIM, _rope64(t3, cosi_ref[...], sini_ref[...]), 0.0)
        kia_ref[...] = ki.astype(kia_ref.dtype)
        kib_ref[...] = pltpu.roll(ki, IDX_DIM, 1).astype(kib_ref.dtype)
        wi_ref[...] = pltpu.roll(t3, IDX_DIM, 1) * (IDX_HEADS ** -0.5 * IDX_DIM ** -0.5)

    acc = jnp.dot(h_ref[...], w_ref[...], preferred_element_type=F32)
    lane_tiles = [slice(t * LANES, (t + 1) * LANES) for t in range(acc.shape[1] // LANES)]

    @pl.when(j < n_q_tiles)
    def _():
        cos = cos_ref[...] * q_scale
        sin = sin_ref[...] * q_scale
        for sl in lane_tiles:
            q_ref[:, sl] = _rope128(acc[:, sl], cos, sin).astype(q_ref.dtype)

    @pl.when(j >= n_q_tiles)
    def _():
        for sl in lane_tiles:
            qi_ref[:, sl] = _rope64(acc[:, sl], cosi_ref[...], sini_ref[...]).astype(qi_ref.dtype)


def _odd_projection(x2, g, mod, w_qqi, w_kvi, cos, sin, cosi, sini, seq, q_width, q_scale, tm=1024, tn=512):
    t, d = x2.shape
    n = w_qqi.shape[1]
    tm = min(tm, seq)
    tn = min(tn, q_width)
    nsb = seq // tm
    n_q_tiles = q_width // tn
    tab = pl.BlockSpec((tm, LANES), lambda i, j: (i, 0))
    shp = lambda w, dt: jax.ShapeDtypeStruct((t, w), dt)
    return pl.pallas_call(
        functools.partial(_odd_proj_kernel, n_q_tiles=n_q_tiles, q_scale=q_scale),
        grid=(t // tm, n // tn),
        in_specs=[pl.BlockSpec((tm, d), lambda i, j: (i, 0)),
                  pl.BlockSpec((1, d), lambda i, j: (0, 0)),
                  pl.BlockSpec((1, 6, d), lambda i, j: (i // nsb, 0, 0)),
                  pl.BlockSpec((d, tn), lambda i, j: (0, j)),
                  pl.BlockSpec(w_kvi.shape, lambda i, j: (0, 0)), tab, tab, tab, tab],
        out_specs=[pl.BlockSpec((tm, tn), lambda i, j: (i, jnp.minimum(j, n_q_tiles - 1))),
                   pl.BlockSpec((tm, tn), lambda i, j: (i, jnp.maximum(j - n_q_tiles, 0))),
                   tab, pl.BlockSpec((tm, 2 * LANES), lambda i, j: (i, 0)), tab, tab, tab],
        out_shape=[shp(q_width, BF16), shp(n - q_width, BF16), shp(LANES, BF16), shp(2 * LANES, BF16),
                   shp(LANES, BF16), shp(LANES, BF16), shp(LANES, F32)],
        scratch_shapes=[pltpu.VMEM((tm, d), BF16)],
        compiler_params=_params(2, VMEM_LIMIT_BYTES),
        name="odd_in_proj",
    )(x2, g.reshape(1, d), mod, w_qqi, w_kvi, cos, sin, cosi, sini)


def _rope_tab_kernel(pos_ref, inv_ref, sgn_ref, cos_ref, sin_ref):
    ang = pos_ref[...].astype(F32) * inv_ref[...]
    cos_ref[...] = jnp.cos(ang)
    sin_ref[...] = jnp.sin(ang) * sgn_ref[...]


def _rope_tables(pos_col, half):
    t = pos_col.shape[0]
    inv = ROPE_THETA ** (-jnp.arange(half, dtype=F32) / half)
    reps = LANES // (2 * half)
    inv_row = jnp.tile(jnp.concatenate([inv, inv]), reps).reshape(1, LANES)
    sgn_row = jnp.tile(jnp.concatenate([-jnp.ones(half, F32), jnp.ones(half, F32)]), reps).reshape(1, LANES)
    tm = min(1024, t)
    row = pl.BlockSpec((1, LANES), lambda i: (0, 0))
    tab = pl.BlockSpec((tm, LANES), lambda i: (i, 0))
    return pl.pallas_call(
        _rope_tab_kernel,
        grid=(t // tm,),
        in_specs=[pl.BlockSpec((tm, 1), lambda i: (i, 0)), row, row],
        out_specs=[tab, tab],
        out_shape=[jax.ShapeDtypeStruct((t, LANES), F32)] * 2,
        compiler_params=_params(1),
        name="rope_tables",
    )(pos_col, inv_row, sgn_row)


def _gate_cumsum_kernel(fg_ref, b_ref, f_ref, carry_ref):
    @pl.when(pl.program_id(1) == 0)
    def _():
        carry_ref[...] = jnp.zeros_like(carry_ref)

    lf = _log_sigmoid(fg_ref[...] + b_ref[...])
    tc = lf.shape[0]
    r = lax.broadcasted_iota(I32, (tc, tc), 0)
    c = lax.broadcasted_iota(I32, (tc, tc), 1)
    tri = (c <= r).astype(BF16)
    cs = carry_ref[...]
    for piece in _split_bf16(lf, 3):
        cs = cs + jnp.dot(tri, piece, preferred_element_type=F32)
    f_ref[...] = cs * LOG2E
    carry_ref[...] = cs[tc - 1:tc, :]


def _gate_cumsum(fg, b_row, batch, seq):
    tc = min(256, seq)
    nsb = seq // tc
    return pl.pallas_call(
        _gate_cumsum_kernel,
        grid=(batch, nsb),
        in_specs=[pl.BlockSpec((tc, LANES), lambda b, j: (b * nsb + j, 0)),
                  pl.BlockSpec((1, LANES), lambda b, j: (0, 0))],
        out_specs=pl.BlockSpec((tc, LANES), lambda b, j: (b * nsb + j, 0)),
        out_shape=jax.ShapeDtypeStruct(fg.shape, F32),
        scratch_shapes=[pltpu.VMEM((1, LANES), F32)],
        compiler_params=_params(2),
        name="gate_cumsum",
    )(fg, b_row)


def _softmax_block(tiles, m_prev):
    mx = functools.reduce(jnp.maximum, tiles)
    m_new = jnp.maximum(m_prev, jnp.broadcast_to(jnp.max(mx, axis=-1, keepdims=True), mx.shape))
    alpha = jnp.exp2(m_prev - m_new)
    return m_new, alpha, [jnp.exp2(t - m_new) for t in tiles]


def _fox_kernel(q_ref, k_ref, v_ref, f_ref, o_ref, m_ref, l_ref, acc_ref, sa_ref, sb_ref, *, tq):
    i = pl.program_id(2)
    m_ref[...] = jnp.full_like(m_ref, -jnp.inf)
    l_ref[...] = jnp.zeros_like(l_ref)
    acc_ref[...] = jnp.zeros_like(acc_ref)
    n_tiles = tq // LANES

    def keys(kj):
        return pl.ds(pl.multiple_of(kj * tq, tq), tq)

    def scores(kj, s_ref):
        s_ref[...] = _dot_nt(q_ref[...], k_ref[keys(kj), :])

    def softmax_pv(kj, s_ref, masked):
        ks = keys(kj)
        s = s_ref[...] - f_ref[0, :, ks]
        if masked:
            row = lax.broadcasted_iota(I32, s.shape, 0)
            col = lax.broadcasted_iota(I32, s.shape, 1)
            s = jnp.where(col <= row, s, -jnp.inf)
        tiles = [s[:, t * LANES:(t + 1) * LANES] for t in range(n_tiles)]
        m_new, alpha, p = _softmax_block(tiles, m_ref[...])
        psum = functools.reduce(jnp.add, p)
        l_ref[...] = alpha * l_ref[...] + jnp.broadcast_to(jnp.sum(psum, axis=-1, keepdims=True), psum.shape)
        pv = jnp.dot(jnp.concatenate(p, axis=-1).astype(BF16), v_ref[ks, :], preferred_element_type=F32)
        acc_ref[...] = alpha * acc_ref[...] + pv
        m_ref[...] = m_new

    scores(0, sa_ref)

    def pair(p, carry):
        scores(2 * p + 1, sb_ref)
        softmax_pv(2 * p, sa_ref, masked=False)
        scores(2 * p + 2, sa_ref)
        softmax_pv(2 * p + 1, sb_ref, masked=False)
        return carry

    lax.fori_loop(0, i // 2, pair, 0)

    @pl.when(i % 2 == 0)
    def _():
        softmax_pv(i, sa_ref, masked=True)

    @pl.when(i % 2 == 1)
    def _():
        scores(i, sb_ref)
        softmax_pv(i - 1, sa_ref, masked=False)
        softmax_pv(i, sb_ref, masked=True)

    o_ref[...] = (acc_ref[...] / l_ref[...]).astype(o_ref.dtype)


def _fox_attention(qkv, f_rows, batch, seq, n_heads, q_col, k_col, v_col):
    tq = min(512, seq)
    nq = seq // tq
    t = qkv.shape[0]
    kernel = functools.partial(_fox_kernel, tq=tq)
    return pl.pallas_call(
        kernel,
        grid=(batch, n_heads, nq),
        in_specs=[pl.BlockSpec((tq, HEAD_DIM), lambda b, h, i: (b * nq + i, q_col + h)),
                  pl.BlockSpec((seq, HEAD_DIM), lambda b, h, i: (b, k_col + h)),
                  pl.BlockSpec((seq, HEAD_DIM), lambda b, h, i: (b, v_col + h)),
                  pl.BlockSpec((1, 1, seq), lambda b, h, i: (b * n_heads + h, 0, 0))],
        out_specs=pl.BlockSpec((tq, HEAD_DIM), lambda b, h, i: (b * nq + i, h)),
        out_shape=jax.ShapeDtypeStruct((t, n_heads * HEAD_DIM), BF16),
        scratch_shapes=[pltpu.VMEM((tq, LANES), F32), pltpu.VMEM((tq, LANES), F32),
                        pltpu.VMEM((tq, HEAD_DIM), F32),
                        pltpu.VMEM((tq, tq), F32), pltpu.VMEM((tq, tq), F32)],
        compiler_params=_params(3, VMEM_LIMIT_BYTES),
        name="fox_attention",
    )(qkv, qkv, qkv, f_rows)


def _sb_kernel(q_ref, k_ref, v_ref, o_ref, c_ref, acc_ref, *, tq, tk, hg):
    i = pl.program_id(2)
    c_ref[...] = jnp.zeros_like(c_ref)
    acc_ref[...] = jnp.zeros_like(acc_ref)
    r = lax.broadcasted_iota(I32, (2 * tk, tk), 0)
    cc = lax.broadcasted_iota(I32, (2 * tk, tk), 1)
    upper2 = ((r & (tk - 1)) > cc).astype(BF16)
    nblk = (i + 1) * (tq // tk)
    col_minus_row = lax.broadcasted_iota(I32, (tq, tk), 1) - lax.broadcasted_iota(I32, (tq, tk), 0)

    def cond(carry):
        step, c_max = carry
        return jnp.logical_and(step < nblk, c_max > SB_EXIT_LOG2)

    def body(carry):
        step, _ = carry
        kj = nblk - 1 - step
        ks = pl.ds(pl.multiple_of(kj * tk, tk), tk)
        strict = col_minus_row < (i * tq - kj * tk)
        c_max = None
        for g in range(hg):
            hd = slice(g * HEAD_DIM, (g + 1) * HEAD_DIM)
            z = _dot_nt(q_ref[:, hd], k_ref[ks, hd])
            ls = jnp.minimum(z, 0.0) - jnp.log2(1.0 + jnp.exp2(-jnp.abs(z)))
            lneg = jnp.where(strict, ls - z, 0.0)
            suffix = jnp.dot(jnp.concatenate(_split_bf16(lneg, 2), axis=1), upper2, preferred_element_type=F32)
            c_prev = c_ref[g]
            a = jnp.where(strict, jnp.exp2(ls + suffix + c_prev), 0.0)
            acc_ref[:, hd] += jnp.dot(a.astype(BF16), v_ref[ks, hd], preferred_element_type=F32)
            c_new = c_prev + suffix[:, 0:1] + lneg[:, 0:1]
            c_ref[g] = c_new
            c_max = jnp.max(c_new) if c_max is None else jnp.maximum(c_max, jnp.max(c_new))
        return step + 1, c_max

    lax.while_loop(cond, body, (jnp.int32(0), jnp.float32(0.0)))
    o_ref[...] = acc_ref[...].astype(o_ref.dtype)


def _sb_attention(qkv, batch, seq, n_heads, q_col, k_col, v_col):
    tq = min(512, seq)
    tk = min(256, seq)
    nq = seq // tq
    t = qkv.shape[0]
    hg = 4 if (n_heads % 4 == 0 and q_col % 4 == 0 and k_col % 4 == 0 and v_col % 4 == 0) else 1
    w = hg * HEAD_DIM
    kernel = functools.partial(_sb_kernel, tq=tq, tk=tk, hg=hg)
    return pl.pallas_call(
        kernel,
        grid=(batch, n_heads // hg, nq),
        in_specs=[pl.BlockSpec((tq, w), lambda b, h, i: (b * nq + i, q_col // hg + h)),
                  pl.BlockSpec((seq, w), lambda b, h, i: (b, k_col // hg + h)),
                  pl.BlockSpec((seq, w), lambda b, h, i: (b, v_col // hg + h))],
        out_specs=pl.BlockSpec((tq, w), lambda b, h, i: (b * nq + i, h)),
        out_shape=jax.ShapeDtypeStruct((t, n_heads * HEAD_DIM), BF16),
        scratch_shapes=[pltpu.VMEM((hg, tq, 1), F32), pltpu.VMEM((tq, w), F32)],
        compiler_params=_params(3, VMEM_LIMIT_BYTES),
        name="sb_attention",
    )(qkv, qkv, qkv)


def _dsa_kernel(q_ref, qi_ref, wi_ref, k_ref, v_ref, kia_ref, kib_ref, o_ref,
                keys_ref, keyst_ref, qs_ref, m_ref, acc_ref, sa_ref, sb_ref, *, n_heads, kc, kca, topk):
    i = pl.program_id(1)
    tq = Q_BLOCK
    nch = ((i + 1) * tq + kc - 1) // kc
    row_t = lax.broadcasted_iota(I32, (tq, kc), 0) + i * tq
    col_l = lax.broadcasted_iota(I32, (tq, kc), 1)

    def index_body(c, carry):
        ks = pl.ds(pl.multiple_of(c * kc, kc), kc)
        kia = kia_ref[ks, :]
        kib = kib_ref[ks, :]
        wi = wi_ref[...]
        score = jnp.zeros((tq, kc), F32)
        for p in range(IDX_HEADS // 2):
            qp = qi_ref[:, p * LANES:(p + 1) * LANES]
            score = score + jnp.maximum(_dot_nt(qp, kia), 0.0) * wi[:, 2 * p:2 * p + 1]
            score = score + jnp.maximum(_dot_nt(qp, kib), 0.0) * wi[:, 2 * p + 1:2 * p + 2]
        adm = ((col_l + c * kc) >> CHUNK_SHIFT) <= (row_t >> CHUNK_SHIFT)
        score = jnp.where(adm, score + 0.0, -jnp.inf)
        keys_ref[:, ks] = score
        keyst_ref[ks, :] = score.T
        return carry

    lax.fori_loop(0, nch, index_body, 0)

    def key_to_float(key):
        return lax.bitcast_convert_type(jnp.where(key >= 0, key, key ^ 0x7FFFFFFF), F32)

    def count_ge(trial_key):
        trial = key_to_float(trial_key)

        def body(c, cnt):
            kk = keyst_ref[pl.ds(pl.multiple_of(c * kc, kc), kc), :]
            hit = jnp.where(kk >= trial, 1.0, 0.0)
            return cnt + jnp.sum(hit.reshape(kc // COUNT_ROWS, COUNT_ROWS, tq), axis=0)
        cnt = lax.fori_loop(0, nch, body, jnp.zeros((COUNT_ROWS, tq), F32))
        return jnp.sum(cnt, axis=0, keepdims=True)

    def per_row(v):
        return jnp.broadcast_to(v, (LANES, tq)).T

    kf = float(topk)
    cur_q = jnp.where(count_ge(jnp.zeros((1, tq), I32)) >= kf, 0, INT_MIN).astype(I32)

    def search_body(it, cur):
        trial = cur + jnp.left_shift(jnp.int32(1), 30 - it)
        return jnp.where(count_ge(trial) >= kf, trial, cur)

    cur_q = lax.fori_loop(0, 31, search_body, cur_q)
    cur = per_row(key_to_float(cur_q))
    thr = per_row(key_to_float(jnp.maximum(cur_q, LOWEST_FINITE_KEY)))

    surplus = jnp.where((count_ge(cur_q) > kf) & (cur_q > INT_MIN), 1.0, 0.0)

    @pl.when(jnp.max(surplus) > 0.0)
    def _():
        need = per_row(kf - count_ge(cur_q + 1))
        r = lax.broadcasted_iota(I32, (kc, kc), 0)
        cc = lax.broadcasted_iota(I32, (kc, kc), 1)
        before = (r < cc).astype(BF16)
        lane_tiles = [slice(t * LANES, (t + 1) * LANES) for t in range(kc // LANES)]

        def tie_body(c, run):
            base = pl.multiple_of(c * kc, kc)
            kk = keys_ref[:, pl.ds(base, kc)]
            eq = [kk[:, sl] == cur for sl in lane_tiles]
            eqf = [jnp.where(e, 1.0, 0.0) for e in eq]
            rank = jnp.dot(jnp.concatenate(eqf, axis=-1).astype(BF16), before, preferred_element_type=F32)
            for t, sl in enumerate(lane_tiles):
                retire = eq[t] & (rank[:, sl] + run >= need)
                keys_ref[:, pl.ds(base + t * LANES, LANES)] = jnp.where(retire, -jnp.inf, kk[:, sl])
            tot = functools.reduce(jnp.add, eqf)
            return run + jnp.broadcast_to(jnp.sum(tot, axis=-1, keepdims=True), tot.shape)

        lax.fori_loop(0, nch, tie_body, jnp.zeros((tq, LANES), F32))

    for h in range(n_heads):
        qs_ref[h * tq:(h + 1) * tq, :] = q_ref[:, h * HEAD_DIM:(h + 1) * HEAD_DIM]
    m_ref[...] = jnp.full_like(m_ref, MASK_BIAS)
    acc_ref[...] = jnp.zeros_like(acc_ref)
    n_att = ((i + 1) * tq + kca - 1) // kca
    n_tiles = kca // LANES
    scored = nch * kc

    def chunk(c):
        return pl.ds(pl.multiple_of(c * kca, kca), kca)

    def scores(c, s_ref):
        s_ref[...] = _dot_nt(qs_ref[...], k_ref[chunk(jnp.minimum(c, n_att - 1)), :])

    def softmax_pv(c, s_ref):
        ks = chunk(c)
        lane = lax.broadcasted_iota(I32, (tq, LANES), 1)
        kk = keys_ref[:, ks]
        s = s_ref[...].reshape(n_heads, tq, kca)
        tiles = []
        for t in range(n_tiles):
            sl = slice(t * LANES, (t + 1) * LANES)
            bias = jnp.where((kk[:, sl] >= thr) & (lane < scored - c * kca - t * LANES), 0.0, MASK_BIAS)
            tiles.append(s[:, :, sl] + bias[None])
        m_new, alpha, p = _softmax_block(tiles, m_ref[...])
        pb = jnp.concatenate(p, axis=-1).reshape(n_heads * tq, kca).astype(BF16)
        pv = jnp.dot(pb, v_ref[ks, :], preferred_element_type=F32)
        a2 = alpha.reshape(n_heads * tq, LANES)
        acc_ref[...] = jnp.concatenate([a2, a2], axis=-1) * acc_ref[...] + pv
        m_ref[...] = m_new

    scores(0, sa_ref)

    def pair(p, carry):
        scores(2 * p + 1, sb_ref)
        softmax_pv(2 * p, sa_ref)
        scores(2 * p + 2, sa_ref)
        softmax_pv(2 * p + 1, sb_ref)
        return carry

    lax.fori_loop(0, n_att // 2, pair, 0)

    @pl.when(n_att % 2 == 1)
    def _():
        softmax_pv(n_att - 1, sa_ref)

    acc = acc_ref[...]
    out = acc[:, :HEAD_DIM] / acc[:, HEAD_DIM:]
    for h in range(n_heads):
        o_ref[:, h * HEAD_DIM:(h + 1) * HEAD_DIM] = out[h * tq:(h + 1) * tq, :].astype(o_ref.dtype)


def _dsa_attention(q, qi, wi, k, v, kia, kib, batch, seq, n_heads):
    tq = Q_BLOCK
    kc = min(256, seq)
    nq = seq // tq
    topk = min(TOPK_MAX, seq // 4)
    t = q.shape[0]
    kca = min(512, seq)
    kernel = functools.partial(_dsa_kernel, n_heads=n_heads, kc=kc, kca=kca, topk=topk)
    qblk = lambda w: pl.BlockSpec((tq, w), lambda b, i: (b * nq + i, 0))
    full = pl.BlockSpec((seq, LANES), lambda b, i: (b, 0))
    vext = pl.BlockSpec((seq, 2 * LANES), lambda b, i: (b, 0))
    return pl.pallas_call(
        kernel,
        grid=(batch, nq),
        in_specs=[qblk(n_heads * HEAD_DIM), qblk(IDX_HEADS * IDX_DIM), qblk(LANES), full, vext, full, full],
        out_specs=qblk(n_heads * HEAD_DIM),
        out_shape=jax.ShapeDtypeStruct((t, n_heads * HEAD_DIM), BF16),
        scratch_shapes=[pltpu.VMEM((tq, seq), F32),
                        pltpu.VMEM((seq, tq), F32),
                        pltpu.VMEM((n_heads * tq, HEAD_DIM), BF16),
                        pltpu.VMEM((n_heads, tq, LANES), F32),
                        pltpu.VMEM((n_heads * tq, 2 * LANES), F32),
                        pltpu.VMEM((n_heads * tq, kca), F32),
                        pltpu.VMEM((n_heads * tq, kca), F32)],
        compiler_params=_params(2, VMEM_LIMIT_BYTES),
        name="dsa_attention",
    )(q, qi, wi, k, v, kia, kib)


def _outproj_kernel(*refs, n_parts, gate_row):
    o_refs = refs[:n_parts]
    w_refs = refs[n_parts:2 * n_parts]
    x_ref, g_ref, mod_ref, out_ref = refs[2 * n_parts:]
    y = jnp.dot(o_refs[0][...], w_refs[0][...], preferred_element_type=F32)
    for o_r, w_r in zip(o_refs[1:], w_refs[1:]):
        y = y + jnp.dot(o_r[...], w_r[...], preferred_element_type=F32)
    out_ref[...] = x_ref[...] + mod_ref[0, gate_row:gate_row + 1, :] * _rms(y, g_ref[...])


def _outproj_residual(o_parts, w_parts, x2, g, mod, seq, gate_row):
    t, d = x2.shape
    tm = min(512, seq)
    nsb = seq // tm
    n_parts = len(o_parts)
    in_specs = [pl.BlockSpec((tm, o.shape[1]), lambda i: (i, 0)) for o in o_parts]
    in_specs += [pl.BlockSpec(w.shape, lambda i: (0, 0)) for w in w_parts]
    in_specs += [pl.BlockSpec((tm, d), lambda i: (i, 0)),
                 pl.BlockSpec((1, d), lambda i: (0, 0)),
                 pl.BlockSpec((1, 6, d), lambda i: (i // nsb, 0, 0))]
    return pl.pallas_call(
        functools.partial(_outproj_kernel, n_parts=n_parts, gate_row=gate_row),
        grid=(t // tm,),
        in_specs=in_specs,
        out_specs=pl.BlockSpec((tm, d), lambda i: (i, 0)),
        out_shape=jax.ShapeDtypeStruct((t, d), F32),
        compiler_params=_params(1, VMEM_LIMIT_BYTES),
        name="outproj_residual",
    )(*o_parts, *w_parts, x2, g.reshape(1, d), mod)


def _ffn_kernel(x_ref, g_in_ref, g_out_ref, mod_ref, w1_ref, w2_ref, out_ref, h_ref, acc_ref):
    j = pl.program_id(1)

    @pl.when(j == 0)
    def _():
        _modnorm_into(h_ref, x_ref, g_in_ref, mod_ref, 3, 4)
        acc_ref[...] = jnp.zeros_like(acc_ref)

    u = jnp.maximum(jnp.dot(h_ref[...], w1_ref[...], preferred_element_type=F32), 0.0)
    acc_ref[...] += jnp.dot((u * u).astype(BF16), w2_ref[...], preferred_element_type=F32)

    @pl.when(j == pl.num_programs(1) - 1)
    def _():
        _gated_residual_into(out_ref, x_ref, acc_ref, g_out_ref, mod_ref, 5)


def _ffn_residual(x2, g_in, g_out, mod, w1, w2, seq):
    t, d = x2.shape
    f = w1.shape[1]
    tm = min(512, seq)
    tf = min(1024, f)
    nsb = seq // tm
    row = pl.BlockSpec((1, d), lambda i, j: (0, 0))
    return pl.pallas_call(
        _ffn_kernel,
        grid=(t // tm, f // tf),
        in_specs=[pl.BlockSpec((tm, d), lambda i, j: (i, 0)), row, row,
                  pl.BlockSpec((1, 6, d), lambda i, j: (i // nsb, 0, 0)),
                  pl.BlockSpec((d, tf), lambda i, j: (0, j)),
                  pl.BlockSpec((tf, d), lambda i, j: (j, 0))],
        out_specs=pl.BlockSpec((tm, d), lambda i, j: (i, 0)),
        out_shape=jax.ShapeDtypeStruct((t, d), F32),
        scratch_shapes=[pltpu.VMEM((tm, d), BF16), pltpu.VMEM((tm, d), F32)],
        compiler_params=_params(2, VMEM_LIMIT_BYTES),
        name="ffn_residual",
    )(x2, g_in.reshape(1, d), g_out.reshape(1, d), mod, w1, w2)


def _even_layer(x2, mod, norm_g, w_in, b_forget, w_out, batch, seq):
    d = x2.shape[1]
    n_heads = d // HEAD_DIM
    n_fox = n_heads // 2
    n_sb = n_heads - n_fox
    fw = n_fox * HEAD_DIM
    sw = n_sb * HEAD_DIM
    scale = HEAD_DIM ** -0.5
    w_main = jnp.concatenate([w_in[:, :fw] * (scale * LOG2E), w_in[:, fw:3 * fw],
                              w_in[:, 3 * fw + n_fox:3 * fw + n_fox + sw] * (scale * LOG2E),
                              w_in[:, 3 * fw + n_fox + sw:]], axis=1).astype(BF16)
    w_gate = jnp.pad(w_in[:, 3 * fw:3 * fw + n_fox], ((0, 0), (0, LANES - n_fox))).astype(BF16)
    b_row = jnp.pad(b_forget.astype(F32), (0, LANES - n_fox)).reshape(1, LANES)

    qkv, fg = _even_projection(x2, norm_g[0], mod, w_main, w_gate, seq)
    f_cum = _gate_cumsum(fg, b_row, batch, seq)
    f_rows = f_cum.reshape(batch, seq, LANES)[:, :, :n_fox].transpose(0, 2, 1).reshape(batch * n_fox, 1, seq)
    o_f = _fox_attention(qkv, f_rows, batch, seq, n_fox, 0, n_fox, 2 * n_fox)
    o_s = _sb_attention(qkv, batch, seq, n_sb, 3 * n_fox, 3 * n_fox + n_sb, 3 * n_fox + 2 * n_sb)
    w_o = w_out.astype(BF16)
    return _outproj_residual([o_f, o_s], [w_o[:fw], w_o[fw:]], x2, norm_g[1], mod, seq, gate_row=2)


def _odd_layer(x2, mod, norm_g, w_in, w_out, pos_col, batch, seq):
    d = x2.shape[1]
    n_heads = d // HEAD_DIM
    qw = n_heads * HEAD_DIM
    iw = IDX_HEADS * IDX_DIM
    o_k, o_v, o_qi, o_ki, o_wi = qw, qw + HEAD_DIM, qw + 2 * HEAD_DIM, qw + 2 * HEAD_DIM + iw, qw + 2 * HEAD_DIM + iw + IDX_DIM
    w_q = w_in[:, :qw].astype(BF16)
    w_qi = w_in[:, o_qi:o_ki].astype(BF16)
    w_kvi = jnp.concatenate([w_in[:, o_k:o_qi], w_in[:, o_ki:],
                             jnp.zeros((d, LANES - IDX_DIM - IDX_HEADS), w_in.dtype)], axis=1).astype(BF16)

    cos, sin = _rope_tables(pos_col, HEAD_DIM // 2)
    cosi, sini = _rope_tables(pos_col, IDX_DIM // 2)
    q, qi, k, v, kia, kib, wi = _odd_projection(
        x2, norm_g[0], mod, jnp.concatenate([w_q, w_qi], axis=1), w_kvi, cos, sin, cosi, sini, seq,
        q_width=qw, q_scale=HEAD_DIM ** -0.5 * LOG2E)
    o = _dsa_attention(q, qi, wi, k, v, kia, kib, batch, seq, n_heads)
    return _outproj_residual([o], [w_out.astype(BF16)], x2, norm_g[1], mod, seq, gate_row=2)


def kernel(x, c, positions, ada_w, ada_b, norm_g, mix_w_out, even_w_in, even_b_forget, odd_w_in, ff_w1, ff_w2):
    batch, seq, d = x.shape
    depth = ada_w.shape[0]
    assert d % HEAD_DIM == 0 and seq % Q_BLOCK == 0 and seq >= TOPK_MAX
    mods = _ada_mod(c, ada_w, ada_b).reshape(depth, batch, 6, d)
    pos_col = positions.reshape(batch * seq, 1)
    x2 = x.reshape(batch * seq, d)
    for l in range(depth):
        mod = mods[l]
        if l % 2 == 0:
            x2 = _even_layer(x2, mod, norm_g[l], even_w_in[l // 2], even_b_forget[l // 2],
                             mix_w_out[l], batch, seq)
        else:
            x2 = _odd_layer(x2, mod, norm_g[l], odd_w_in[l // 2], mix_w_out[l], pos_col, batch, seq)
        x2 = _ffn_residual(x2, norm_g[l, 2], norm_g[l, 3], mod,
                           ff_w1[l].astype(BF16), ff_w2[l].astype(BF16), seq)
    return x2.reshape(batch, seq, d)
```
